```python
import jax, jax.numpy as jnp
from jax import lax
import numpy as np

D_MODEL = 2048
BATCH = 2
SEQ = 16384
DEPTH = 1

GRID_W = 64
MIX_DIM = D_MODEL
CONV_DIM = MIX_DIM // 2
NA_HEADS = 16
NA_HEAD_DIM = 64
NA_DIM = NA_HEADS * NA_HEAD_DIM
CONV_K = 31
WIN_H = 8
WIN_W = 16
IN_COLS = 2 * CONV_DIM + 3 * NA_DIM
D_FF = 5632
FFN_K = 3
EPS = 1e-6

kernel_name = "hybrid_conformer_conv_natten_convffn"


def rms_norm(x, g):
    xf = x.astype(jnp.float32)
    y = xf * lax.rsqrt(jnp.mean(xf * xf, axis=-1, keepdims=True) + EPS)
    return (y * g.astype(jnp.float32)).astype(x.dtype)


def layer_norm(x, g, b):
    xf = x.astype(jnp.float32)
    mu = jnp.mean(xf, axis=-1, keepdims=True)
    var = jnp.mean(jnp.square(xf - mu), axis=-1, keepdims=True)
    y = (xf - mu) * lax.rsqrt(var + EPS)
    return (y * g.astype(jnp.float32) + b.astype(jnp.float32)).astype(x.dtype)


def depthwise_conv(x, w, b):
    k = w.shape[0]
    c = x.shape[-1]
    y = lax.conv_general_dilated(
        x, w[:, None, :].astype(x.dtype), window_strides=(1,),
        padding=[(k // 2, k // 2)], dimension_numbers=("NWC", "WIO", "NWC"),
        feature_group_count=c)
    return y + b.astype(x.dtype)


def conformer_conv(u_val, u_gate, dw_w, dw_b, ln_g, ln_b):
    h = u_val * jax.nn.sigmoid(u_gate)
    h = depthwise_conv(h, dw_w, dw_b)
    h = layer_norm(h, ln_g, ln_b)
    return jax.nn.silu(h)


def neighborhood_attention(q, k, v, rpb):
    b, s, h, dh = q.shape
    rows = s // GRID_W
    kh = min(WIN_H, rows)
    kw = WIN_W
    qg = q.reshape(b, rows, GRID_W, h, dh) * (dh ** -0.5)
    kg = k.reshape(b, rows, GRID_W, h, dh)
    vg = v.reshape(b, rows, GRID_W, h, dh)
    cols = np.arange(GRID_W)
    col_start = np.clip(cols - kw // 2, 0, GRID_W - kw)
    col_idx = col_start[:, None] + np.arange(kw)[None, :]
    dc = col_idx - cols[:, None] + (WIN_W - 1)
    rpb_cols = rpb[:, :, dc].astype(jnp.float32)

    def row_block(r):
        rs = jnp.clip(r - kh // 2, 0, rows - kh)
        k_rows = lax.dynamic_slice_in_dim(kg, rs, kh, axis=1)
        v_rows = lax.dynamic_slice_in_dim(vg, rs, kh, axis=1)
        k_win = k_rows[:, :, col_idx]
        v_win = v_rows[:, :, col_idx]
        q_row = lax.dynamic_index_in_dim(qg, r, axis=1, keepdims=False)
        scores = jnp.einsum("bchd,bicjhd->bchij", q_row, k_win).astype(jnp.float32)
        dr = rs + jnp.arange(kh) - r + (WIN_H - 1)
        bias = rpb_cols[:, dr]
        scores = scores + jnp.transpose(bias, (2, 0, 1, 3))[None]
        p = jax.nn.softmax(scores.reshape(b, GRID_W, h, kh * kw), axis=-1)
        p = p.reshape(b, GRID_W, h, kh, kw).astype(v.dtype)
        return jnp.einsum("bchij,bicjhd->bchd", p, v_win)

    out = lax.map(row_block, jnp.arange(rows))
    return jnp.transpose(out, (1, 0, 2, 3, 4)).reshape(b, s, h * dh)


def setup_inputs(seed: int = 0) -> dict:
    key = jax.random.key(seed)
    ks = jax.random.split(key, 18)
    L = DEPTH

    def nrm(k, shape, scale):
        return jax.random.normal(k, shape, jnp.float32) * scale

    return {
        "x": nrm(ks[0], (BATCH, SEQ, D_MODEL), 1.0),
        "attn_norm_g": 1.0 + nrm(ks[1], (L, D_MODEL), 0.02),
        "w_in": nrm(ks[2], (L, D_MODEL, IN_COLS), D_MODEL ** -0.5),
        "conv_dw_w": nrm(ks[3], (L, CONV_K, CONV_DIM), CONV_K ** -0.5),
        "conv_dw_b": nrm(ks[4], (L, CONV_DIM), 0.02),
        "conv_ln_g": 1.0 + nrm(ks[5], (L, CONV_DIM), 0.02),
        "conv_ln_b": nrm(ks[6], (L, CONV_DIM), 0.02),
        "rpb": nrm(ks[7], (L, NA_HEADS, 2 * WIN_H - 1, 2 * WIN_W - 1), 0.1),
        "conv_out_g": 1.0 + nrm(ks[8], (L, CONV_DIM), 0.02),
        "na_out_g": 1.0 + nrm(ks[9], (L, NA_DIM), 0.02),
        "w_out": nrm(ks[10], (L, MIX_DIM, D_MODEL), MIX_DIM ** -0.5),
        "ffn_norm_g": 1.0 + nrm(ks[11], (L, D_MODEL), 0.02),
        "w_up": nrm(ks[12], (L, D_MODEL, 2 * D_FF), D_MODEL ** -0.5),
        "ffn_dw_w": nrm(ks[13], (L, FFN_K, 2 * D_FF), FFN_K ** -0.5),
        "ffn_dw_b": nrm(ks[14], (L, 2 * D_FF), 0.02),
        "w_down": nrm(ks[15], (L, D_FF, D_MODEL), D_FF ** -0.5),
        "final_norm_g": 1.0 + nrm(ks[16], (D_MODEL,), 0.02),
    }


def reference(x, attn_norm_g, w_in, conv_dw_w, conv_dw_b, conv_ln_g, conv_ln_b,
              rpb, conv_out_g, na_out_g, w_out, ffn_norm_g, w_up, ffn_dw_w,
              ffn_dw_b, w_down, final_norm_g):
    b, s, _ = x.shape
    split_pts = [CONV_DIM, 2 * CONV_DIM, 2 * CONV_DIM + NA_DIM, 2 * CONV_DIM + 2 * NA_DIM]
    for l in range(DEPTH):
        xn = rms_norm(x, attn_norm_g[l])
        proj = jnp.einsum("bsd,dc->bsc", xn, w_in[l])
        u_val, u_gate, q, k, v = jnp.split(proj, split_pts, axis=-1)
        y_conv = conformer_conv(u_val, u_gate, conv_dw_w[l], conv_dw_b[l],
                                conv_ln_g[l], conv_ln_b[l])
        heads = lambda t: t.reshape(b, s, NA_HEADS, NA_HEAD_DIM)
        y_na = neighborhood_attention(heads(q), heads(k), heads(v), rpb[l])
        mixed = jnp.concatenate([rms_norm(y_conv, conv_out_g[l]),
                                 rms_norm(y_na, na_out_g[l])], axis=-1)
        x = x + jnp.einsum("bsc,cd->bsd", mixed, w_out[l])
        xn = rms_norm(x, ffn_norm_g[l])
        up = jnp.einsum("bsd,df->bsf", xn, w_up[l])
        up = depthwise_conv(up, ffn_dw_w[l], ffn_dw_b[l])
        gate, val = jnp.split(up, 2, axis=-1)
        x = x + jnp.einsum("bsf,fd->bsd", jax.nn.silu(gate) * val, w_down[l])
    return rms_norm(x, final_norm_g)
```

```python
import functools

import numpy as np
import jax
import jax.numpy as jnp
from jax import lax
from jax.experimental import pallas as pl
from jax.experimental.pallas import tpu as pltpu

GRID_W = 64
WIN_H = 8
WIN_W = 16
HEAD_DIM = 64
EPS = 1e-6
MASKED = -1e30

LANES = 128
SUBLANES = 8
HALO = 16
VMEM_LIMIT = 56 * 1024 * 1024

F32 = jnp.float32
BF16 = jnp.bfloat16


def _pick(n, candidates):
    for c in candidates:
        if n % c == 0:
            return c
    raise ValueError(f"no tile in {candidates} divides {n}")


def _params(*semantics):
    return pltpu.CompilerParams(dimension_semantics=semantics, vmem_limit_bytes=VMEM_LIMIT)


def _rms(x, g):
    return x * lax.rsqrt(jnp.mean(x * x, axis=-1, keepdims=True) + EPS) * g


def _halo_specs(tm, width, n_rows):
    per = tm // HALO
    last = n_rows // HALO - 1
    prev = pl.BlockSpec((HALO, width), lambda i, *_: (jnp.maximum(i * per - 1, 0), 0))
    nxt = pl.BlockSpec((HALO, width), lambda i, *_: (jnp.minimum((i + 1) * per, last), 0))
    return prev, nxt


def _in_proj_kernel(x_ref, g_ref, w_ref, o_ref, xn_ref):
    @pl.when(pl.program_id(1) == 0)
    def _():
        xn_ref[...] = _rms(x_ref[...], g_ref[...]).astype(BF16)

    o_ref[...] = jnp.dot(xn_ref[...], w_ref[...], preferred_element_type=F32).astype(o_ref.dtype)


def _in_proj(x2, g, w):
    t, d = x2.shape
    n = w.shape[1]
    tm = _pick(t, (512, 256, 128))
    tn = _pick(n, (1280, 1024, 640, 512, 256, 128))
    return pl.pallas_call(
        _in_proj_kernel,
        grid=(t // tm, n // tn),
        in_specs=[pl.BlockSpec((tm, d), lambda i, j: (i, 0)),
                  pl.BlockSpec((1, d), lambda i, j: (0, 0)),
                  pl.BlockSpec((d, tn), lambda i, j: (0, j))],
        out_specs=pl.BlockSpec((tm, tn), lambda i, j: (i, j)),
        out_shape=jax.ShapeDtypeStruct((t, n), BF16),
        scratch_shapes=[pltpu.VMEM((tm, d), BF16)],
        compiler_params=_params("parallel", "arbitrary"),
        name="in_proj",
    )(x2, g, w)


def _conv_kernel(prev_ref, main_ref, next_ref, dww_ref, dwb_ref, lng_ref, lnb_ref, og_ref,
                 o_ref, h_ref, sh_ref, c_ref, *, tm, tiles_per_seq, c_dim, taps, row_chunk):
    i = pl.program_id(0)
    at_start = (i % tiles_per_seq) == 0
    at_end = (i % tiles_per_seq) == tiles_per_seq - 1
    pad = taps // 2

    def glu(u):
        u = u.astype(F32)
        return u[:, :c_dim] * jax.nn.sigmoid(u[:, c_dim:])

    h_ref[0:HALO, :] = jnp.where(at_start, 0.0, glu(prev_ref[...]))
    h_ref[HALO:HALO + tm, :] = glu(main_ref[...])
    h_ref[HALO + tm:, :] = jnp.where(at_end, 0.0, glu(next_ref[...]))

    n_sh = sh_ref.shape[1]
    base = HALO - pad
    for lc in range(c_dim // LANES):
        lanes = pl.ds(lc * LANES, LANES)
        for b in range(SUBLANES):
            sh_ref[b] = h_ref[pl.ds(b, n_sh), lanes]

        def rows(rc, carry):
            r0 = pl.multiple_of(rc * row_chunk, row_chunk)
            acc = jnp.broadcast_to(dwb_ref[:, lanes], (row_chunk, LANES))
            for k in range(taps):
                a, b = divmod(base + k, SUBLANES)
                acc = acc + dww_ref[k:k + 1, lanes] * sh_ref[b, pl.ds(r0 + a * SUBLANES, row_chunk), :]
            c_ref[pl.ds(r0, row_chunk), lanes] = acc
            return carry

        lax.fori_loop(0, tm // row_chunk, rows, 0)

    ln_rows = min(tm, 128)

    def norm(rc, carry):
        r0 = pl.multiple_of(rc * ln_rows, ln_rows)
        c = c_ref[pl.ds(r0, ln_rows), :]
        mu = jnp.mean(c, axis=-1, keepdims=True)
        d = c - mu
        var = jnp.mean(d * d, axis=-1, keepdims=True)
        y = d * lax.rsqrt(var + EPS) * lng_ref[...] + lnb_ref[...]
        s = y * jax.nn.sigmoid(y)
        o_ref[pl.ds(r0, ln_rows), :] = _rms(s, og_ref[...]).astype(o_ref.dtype)
        return carry

    lax.fori_loop(0, tm // ln_rows, norm, 0)


def _conv_branch(proj, dww, dwb, lng, lnb, og, seq):
    t = proj.shape[0]
    taps, c_dim = dww.shape
    assert taps // 2 < HALO and c_dim % LANES == 0
    tm = _pick(seq, (512, 256, 128))
    row_chunk = 64
    n_sh = tm + (HALO + taps // 2) // SUBLANES * SUBLANES
    prev, nxt = _halo_specs(tm, 2 * c_dim, t)
    vec = pl.BlockSpec((1, c_dim), lambda i: (0, 0))
    kern = functools.partial(_conv_kernel, tm=tm, tiles_per_seq=seq // tm, c_dim=c_dim,
                             taps=taps, row_chunk=row_chunk)
    return pl.pallas_call(
        kern,
        grid=(t // tm,),
        in_specs=[prev, pl.BlockSpec((tm, 2 * c_dim), lambda i: (i, 0)), nxt,
                  pl.BlockSpec((taps, c_dim), lambda i: (0, 0)), vec, vec, vec, vec],
        out_specs=pl.BlockSpec((tm, c_dim), lambda i: (i, 0)),
        out_shape=jax.ShapeDtypeStruct((t, c_dim), BF16),
        scratch_shapes=[pltpu.VMEM((tm + 2 * HALO, c_dim), F32),
                        pltpu.VMEM((SUBLANES, n_sh, LANES), F32),
                        pltpu.VMEM((tm, c_dim), F32)],
        compiler_params=_params("parallel"),
        name="conv_branch",
    )(proj, proj, proj, dww, dwb, lng, lnb, og)


def _bias_table(rpb):
    n_heads = rpb.shape[0]
    cols = np.arange(GRID_W)
    col_start = np.clip(cols - WIN_W // 2, 0, GRID_W - WIN_W)
    kc = np.arange(GRID_W)
    rel = kc[None, :] - cols[:, None] + (WIN_W - 1)
    valid = (kc[None, :] >= col_start[:, None]) & (kc[None, :] < col_start[:, None] + WIN_W)
    t = rpb.astype(F32)[:, :, np.clip(rel, 0, 2 * WIN_W - 2)]
    t = jnp.where(valid[None, None], t, MASKED)
    tabs = [jnp.transpose(t[:, d:d + WIN_H], (0, 2, 1, 3)).reshape(n_heads * GRID_W, WIN_H * GRID_W)
            for d in range(WIN_H)]
    return jnp.stack(tabs)


def _natten_kernel(q_ref, k_ref, v_ref, bias_ref, o_ref, *, rows, rows_per_step):
    c = pl.program_id(2)
    lane = lax.broadcasted_iota(jnp.int32, (GRID_W, LANES), 1)
    first = lane < HEAD_DIM
    scale = HEAD_DIM ** -0.5
    n_keys = WIN_H * GRID_W

    def row(rr, carry):
        r = c * rows_per_step + rr
        rs = jnp.clip(r - WIN_H // 2, 0, rows - WIN_H)
        d = rs - r + (WIN_H - 1)
        q = q_ref[pl.ds(pl.multiple_of(rr * GRID_W, GRID_W), GRID_W), :] * scale
        zero = jnp.zeros_like(q)
        q2 = jnp.concatenate([jnp.where(first, q, zero), jnp.where(first, zero, q)], axis=0)
        kw = k_ref[pl.ds(pl.multiple_of(rs * GRID_W, GRID_W), n_keys), :]
        vw = v_ref[pl.ds(pl.multiple_of(rs * GRID_W, GRID_W), n_keys), :]
        s = lax.dot_general(q2, kw, (((1,), (1,)), ((), ())), preferred_element_type=F32)
        s = s + bias_ref[d]
        m = jnp.max(s, axis=-1, keepdims=True)
        p = jnp.exp(s - m)
        l = jnp.sum(p, axis=-1, keepdims=True)
        o = jnp.dot(p.astype(BF16), vw, preferred_element_type=F32) / l
        out = jnp.where(first, o[:GRID_W], o[GRID_W:])
        o_ref[pl.ds(pl.multiple_of(rr * GRID_W, GRID_W), GRID_W), :] = out.astype(o_ref.dtype)
        return carry

    lax.fori_loop(0, rows_per_step, row, 0)


def _natten(proj, bias, batch, seq, q_col, k_col, v_col, na_dim):
    t = proj.shape[0]
    rows = seq // GRID_W
    assert rows >= WIN_H and na_dim % LANES == 0 and LANES == 2 * HEAD_DIM
    pairs = na_dim // LANES
    rows_per_step = _pick(rows, (32, 16, 8))
    tq = rows_per_step * GRID_W
    chunks = seq // tq
    kern = functools.partial(_natten_kernel, rows=rows, rows_per_step=rows_per_step)
    return pl.pallas_call(
        kern,
        grid=(batch, pairs, chunks),
        in_specs=[pl.BlockSpec((tq, LANES), lambda b, p, c: (b * chunks + c, q_col + p)),
                  pl.BlockSpec((seq, LANES), lambda b, p, c: (b, k_col + p)),
                  pl.BlockSpec((seq, LANES), lambda b, p, c: (b, v_col + p)),
                  pl.BlockSpec((WIN_H, LANES, WIN_H * GRID_W), lambda b, p, c: (0, p, 0))],
        out_specs=pl.BlockSpec((tq, LANES), lambda b, p, c: (b * chunks + c, p)),
        out_shape=jax.ShapeDtypeStruct((t, na_dim), BF16),
        compiler_params=_params("parallel", "parallel", "arbitrary"),
        name="natten",
    )(proj, proj, proj, bias)


def _out_proj_kernel(yc_ref, yn_ref, x_ref, nag_ref, w_ref, fg_ref, x1_ref, xn_ref, *, c_dim):
    na = _rms(yn_ref[...].astype(F32), nag_ref[...]).astype(BF16)
    acc = jnp.dot(yc_ref[...], w_ref[0:c_dim, :], preferred_element_type=F32)
    acc = acc + jnp.dot(na, w_ref[c_dim:, :], preferred_element_type=F32)
    x1 = x_ref[...] + acc
    x1_ref[...] = x1
    xn_ref[...] = _rms(x1, fg_ref[...]).astype(xn_ref.dtype)


def _out_proj(y_conv, y_na, x2, na_g, w_out, ffn_g):
    t, d = x2.shape
    c_dim, na_dim = y_conv.shape[1], y_na.shape[1]
    tm = _pick(t, (256, 128))
    kern = functools.partial(_out_proj_kernel, c_dim=c_dim)
    return pl.pallas_call(
        kern,
        grid=(t // tm,),
        in_specs=[pl.BlockSpec((tm, c_dim), lambda i: (i, 0)),
                  pl.BlockSpec((tm, na_dim), lambda i: (i, 0)),
                  pl.BlockSpec((tm, d), lambda i: (i, 0)),
                  pl.BlockSpec((1, na_dim), lambda i: (0, 0)),
                  pl.BlockSpec((c_dim + na_dim, d), lambda i: (0, 0)),
                  pl.BlockSpec((1, d), lambda i: (0, 0))],
        out_specs=[pl.BlockSpec((tm, d), lambda i: (i, 0)),
                   pl.BlockSpec((tm, d), lambda i: (i, 0))],
        out_shape=[jax.ShapeDtypeStruct((t, d), F32), jax.ShapeDtypeStruct((t, d), BF16)],
        compiler_params=_params("parallel"),
        name="out_proj",
    )(y_conv, y_na, x2, na_g, w_out, ffn_g)


def _ffn_kernel(prev_ref, main_ref, next_ref, wg_ref, wv_ref, dwg_ref, dwv_ref, bg_ref, bv_ref,
                wd_ref, x1_ref, fng_ref, o_ref, lhs_ref, u_ref, acc_ref,
                *, tm, tiles_per_seq, final_norm):
    i = pl.program_id(0)
    j = pl.program_id(1)

    @pl.when(j == 0)
    def _():
        at_start = (i % tiles_per_seq) == 0
        at_end = (i % tiles_per_seq) == tiles_per_seq - 1
        lhs_ref[0:HALO, :] = jnp.where(at_start, jnp.zeros_like(prev_ref), prev_ref[...])
        lhs_ref[HALO:HALO + tm, :] = main_ref[...]
        lhs_ref[HALO + tm:, :] = jnp.where(at_end, jnp.zeros_like(next_ref), next_ref[...])
        acc_ref[...] = jnp.zeros_like(acc_ref)

    def branch(w_ref, dw_ref, b_ref):
        u_ref[...] = jnp.dot(lhs_ref[...], w_ref[...], preferred_element_type=F32)
        return (dw_ref[0:1, :] * u_ref[pl.ds(HALO - 1, tm), :]
                + dw_ref[1:2, :] * u_ref[pl.ds(HALO, tm), :]
                + dw_ref[2:3, :] * u_ref[pl.ds(HALO + 1, tm), :] + b_ref[...])

    gate = branch(wg_ref, dwg_ref, bg_ref)
    val = branch(wv_ref, dwv_ref, bv_ref)
    h = (gate * jax.nn.sigmoid(gate) * val).astype(BF16)
    acc_ref[...] += jnp.dot(h, wd_ref[...], preferred_element_type=F32)

    @pl.when(j == pl.num_programs(1) - 1)
    def _():
        y = x1_ref[...] + acc_ref[...]
        o_ref[...] = _rms(y, fng_ref[...]) if final_norm else y


def _ffn(xn, x1, w_up, dw_w, dw_b, w_down, fn_g, seq, final_norm):
    t, d = xn.shape
    d_ff = w_down.shape[0]
    assert dw_w.shape[0] == 3
    tm = _pick(seq, (512, 256, 128))
    tf = _pick(d_ff, (512, 256, 128))
    nf = d_ff // tf
    prev, nxt = _halo_specs(tm, d, t)
    kern = functools.partial(_ffn_kernel, tm=tm, tiles_per_seq=seq // tm, final_norm=final_norm)
    return pl.pallas_call(
        kern,
        grid=(t // tm, nf),
        in_specs=[prev, pl.BlockSpec((tm, d), lambda i, j: (i, 0)), nxt,
                  pl.BlockSpec((d, tf), lambda i, j: (0, j)),
                  pl.BlockSpec((d, tf), lambda i, j: (0, j + nf)),
                  pl.BlockSpec((3, tf), lambda i, j: (0, j)),
                  pl.BlockSpec((3, tf), lambda i, j: (0, j + nf)),
                  pl.BlockSpec((1, tf), lambda i, j: (0, j)),
                  pl.BlockSpec((1, tf), lambda i, j: (0, j + nf)),
                  pl.BlockSpec((tf, d), lambda i, j: (j, 0)),
                  pl.BlockSpec((tm, d), lambda i, j: (i, 0)),
                  pl.BlockSpec((1, d), lambda i, j: (0, 0))],
        out_specs=pl.BlockSpec((tm, d), lambda i, j: (i, 0)),
        out_shape=jax.ShapeDtypeStruct((t, d), F32),
        scratch_shapes=[pltpu.VMEM((tm + 2 * HALO, d), BF16),
                        pltpu.VMEM((tm + 2 * HALO, tf), F32),
                        pltpu.VMEM((tm, d), F32)],
        compiler_params=_params("parallel", "arbitrary"),
        name="conv_ffn",
    )(xn, xn, xn, w_up, w_up, dw_w, dw_w, dw_b, dw_b, w_down, x1, fn_g)


def kernel(x, attn_norm_g, w_in, conv_dw_w, conv_dw_b, conv_ln_g, conv_ln_b, rpb, conv_out_g,
           na_out_g, w_out, ffn_norm_g, w_up, ffn_dw_w, ffn_dw_b, w_down, final_norm_g):
    batch, seq, d = x.shape
    depth = w_in.shape[0]
    c_dim = conv_dw_w.shape[-1]
    na_dim = na_out_g.shape[-1]
    assert seq % GRID_W == 0 and w_in.shape[-1] == 2 * c_dim + 3 * na_dim
    q_col = 2 * c_dim // LANES
    k_col = q_col + na_dim // LANES
    v_col = k_col + na_dim // LANES
    row = lambda a: a.reshape(1, -1).astype(F32)

    x2 = x.reshape(batch * seq, d)
    for l in range(depth):
        proj = _in_proj(x2, row(attn_norm_g[l]), w_in[l].astype(BF16))
        y_conv = _conv_branch(proj, conv_dw_w[l], row(conv_dw_b[l]), row(conv_ln_g[l]),
                              row(conv_ln_b[l]), row(conv_out_g[l]), seq)
        y_na = _natten(proj, _bias_table(rpb[l]), batch, seq, q_col, k_col, v_col, na_dim)
        x1, xn = _out_proj(y_conv, y_na, x2, row(na_out_g[l]), w_out[l].astype(BF16),
                           row(ffn_norm_g[l]))
        last = l == depth - 1
        x2 = _ffn(xn, x1, w_up[l].astype(BF16), ffn_dw_w[l], row(ffn_dw_b[l]),
                  w_down[l].astype(BF16), row(final_norm_g), seq, final_norm=last)
    return x2.reshape(batch, seq, d)
```

```python
import functools

import numpy as np
import jax
import jax.numpy as jnp
from jax import lax
from jax.experimental import pallas as pl
from jax.experimental.pallas import tpu as pltpu

GRID_W = 64
WIN_H = 8
WIN_W = 16
HEAD_DIM = 64
EPS = 1e-6
MASKED = -1e30

LANES = 128
SUBLANES = 8
HALO = 16
VMEM_LIMIT = 56 * 1024 * 1024

F32 = jnp.float32
BF16 = jnp.bfloat16


def _pick(n, candidates):
    for c in candidates:
        if n % c == 0:
            return c
    raise ValueError(f"no tile in {candidates} divides {n}")


def _params(*semantics):
    return pltpu.CompilerParams(dimension_semantics=semantics, vmem_limit_bytes=VMEM_LIMIT)


def _rms(x, g):
    return x * lax.rsqrt(jnp.mean(x * x, axis=-1, keepdims=True) + EPS) * g


def _halo_specs(tm, width, n_rows):
    per = tm // HALO
    last = n_rows // HALO - 1
    prev = pl.BlockSpec((HALO, width), lambda i, *_: (jnp.maximum(i * per - 1, 0), 0))
    nxt = pl.BlockSpec((HALO, width), lambda i, *_: (jnp.minimum((i + 1) * per, last), 0))
    return prev, nxt


def _in_proj_kernel(x_ref, g_ref, w_ref, o_ref, xn_ref):
    @pl.when(pl.program_id(1) == 0)
    def _():
        xn_ref[...] = _rms(x_ref[...], g_ref[...]).astype(BF16)

    o_ref[...] = jnp.dot(xn_ref[...], w_ref[...], preferred_element_type=F32).astype(o_ref.dtype)


def _in_proj(x2, g, w):
    t, d = x2.shape
    n = w.shape[1]
    tm = _pick(t, (512, 256, 128))
    tn = _pick(n, (1280, 1024, 640, 512, 256, 128))
    return pl.pallas_call(
        _in_proj_kernel,
        grid=(t // tm, n // tn),
        in_specs=[pl.BlockSpec((tm, d), lambda i, j: (i, 0)),
                  pl.BlockSpec((1, d), lambda i, j: (0, 0)),
                  pl.BlockSpec((d, tn), lambda i, j: (0, j))],
        out_specs=pl.BlockSpec((tm, tn), lambda i, j: (i, j)),
        out_shape=jax.ShapeDtypeStruct((t, n), BF16),
        scratch_shapes=[pltpu.VMEM((tm, d), BF16)],
        compiler_params=_params("parallel", "arbitrary"),
        name="in_proj",
    )(x2, g, w)


def _conv_kernel(prev_ref, main_ref, next_ref, dww_ref, dwb_ref, lng_ref, lnb_ref, og_ref,
                 o_ref, h_ref, sh_ref, c_ref, *, tm, tiles_per_seq, c_dim, taps, row_chunk):
    i = pl.program_id(0)
    at_start = (i % tiles_per_seq) == 0
    at_end = (i % tiles_per_seq) == tiles_per_seq - 1
    pad = taps // 2

    def glu(u):
        u = u.astype(F32)
        return u[:, :c_dim] * jax.nn.sigmoid(u[:, c_dim:])

    h_ref[0:HALO, :] = jnp.where(at_start, 0.0, glu(prev_ref[...]))
    h_ref[HALO:HALO + tm, :] = glu(main_ref[...])
    h_ref[HALO + tm:, :] = jnp.where(at_end, 0.0, glu(next_ref[...]))

    n_sh = sh_ref.shape[1]
    base = HALO - pad
    for lc in range(c_dim // LANES):
        lanes = pl.ds(lc * LANES, LANES)
        for b in range(SUBLANES):
            sh_ref[b] = h_ref[pl.ds(b, n_sh), lanes]

        def rows(rc, carry):
            r0 = pl.multiple_of(rc * row_chunk, row_chunk)
            acc = jnp.broadcast_to(dwb_ref[:, lanes], (row_chunk, LANES))
            for k in range(taps):
                a, b = divmod(base + k, SUBLANES)
                acc = acc + dww_ref[k:k + 1, lanes] * sh_ref[b, pl.ds(r0 + a * SUBLANES, row_chunk), :]
            c_ref[pl.ds(r0, row_chunk), lanes] = acc
            return carry

        lax.fori_loop(0, tm // row_chunk, rows, 0)

    ln_rows = min(tm, 128)

    def norm(rc, carry):
        r0 = pl.multiple_of(rc * ln_rows, ln_rows)
        c = c_ref[pl.ds(r0, ln_rows), :]
        mu = jnp.mean(c, axis=-1, keepdims=True)
        d = c - mu
        var = jnp.mean(d * d, axis=-1, keepdims=True)
        y = d * lax.rsqrt(var + EPS) * lng_ref[...] + lnb_ref[...]
        s = y * jax.nn.sigmoid(y)
        o_ref[pl.ds(r0, ln_rows), :] = _rms(s, og_ref[...]).astype(o_ref.dtype)
        return carry

    lax.fori_loop(0, tm // ln_rows, norm, 0)


def _conv_branch(proj, dww, dwb, lng, lnb, og, seq):
    t = proj.shape[0]
    taps, c_dim = dww.shape
    assert taps // 2 < HALO and c_dim % LANES == 0
    tm = _pick(seq, (512, 256, 128))
    row_chunk = 64
    n_sh = tm + (HALO + taps // 2) // SUBLANES * SUBLANES
    prev, nxt = _halo_specs(tm, 2 * c_dim, t)
    vec = pl.BlockSpec((1, c_dim), lambda i: (0, 0))
    kern = functools.partial(_conv_kernel, tm=tm, tiles_per_seq=seq // tm, c_dim=c_dim,
                             taps=taps, row_chunk=row_chunk)
    return pl.pallas_call(
        kern,
        grid=(t // tm,),
        in_specs=[prev, pl.BlockSpec((tm, 2 * c_dim), lambda i: (i, 0)), nxt,
                  pl.BlockSpec((taps, c_dim), lambda i: (0, 0)), vec, vec, vec, vec],
        out_specs=pl.BlockSpec((tm, c_dim), lambda i: (i, 0)),
        out_shape=jax.ShapeDtypeStruct((t, c_dim), BF16),
        scratch_shapes=[pltpu.VMEM((tm + 2 * HALO, c_dim), F32),
                        pltpu.VMEM((SUBLANES, n_sh, LANES), F32),
                        pltpu.VMEM((tm, c_dim), F32)],
        compiler_params=_params("parallel"),
        name="conv_branch",
    )(proj, proj, proj, dww, dwb, lng, lnb, og)


def _bias_table(rpb):
    n_heads = rpb.shape[0]
    cols = np.arange(GRID_W)
    col_start = np.clip(cols - WIN_W // 2, 0, GRID_W - WIN_W)
    kc = np.arange(GRID_W)
    rel = kc[None, :] - cols[:, None] + (WIN_W - 1)
    valid = (kc[None, :] >= col_start[:, None]) & (kc[None, :] < col_start[:, None] + WIN_W)
    t = rpb.astype(F32)[:, :, np.clip(rel, 0, 2 * WIN_W - 2)]
    t = jnp.where(valid[None, None], t, MASKED)
    tabs = [jnp.transpose(t[:, d:d + WIN_H], (0, 2, 1, 3)).reshape(n_heads * GRID_W, WIN_H * GRID_W)
            for d in range(WIN_H)]
    return jnp.stack(tabs)


def _natten_kernel(q_ref, k_ref, v_ref, bias_ref, o_ref, *, rows, rows_per_step, rows_per_group):
    c = pl.program_id(2)
    lane = lax.broadcasted_iota(jnp.int32, (GRID_W, LANES), 1)
    first = lane < HEAD_DIM
    scale = HEAD_DIM ** -0.5
    n_keys = WIN_H * GRID_W

    def scores(rr):
        r = c * rows_per_step + rr
        rs = jnp.clip(r - WIN_H // 2, 0, rows - WIN_H)
        d = rs - r + (WIN_H - 1)
        q = q_ref[pl.ds(pl.multiple_of(rr * GRID_W, GRID_W), GRID_W), :] * scale
        zero = jnp.zeros_like(q)
        q2 = jnp.concatenate([jnp.where(first, q, zero), jnp.where(first, zero, q)], axis=0)
        kw = k_ref[pl.ds(pl.multiple_of(rs * GRID_W, GRID_W), n_keys), :]
        s = lax.dot_general(q2, kw, (((1,), (1,)), ((), ())), preferred_element_type=F32)
        return s + bias_ref[d], rs

    def softmax(s):
        m = jnp.max(s, axis=-1, keepdims=True)
        p = jnp.exp(s - m)
        return p.astype(BF16), jnp.sum(p, axis=-1, keepdims=True)

    def group(g, carry):
        rrs = [g * rows_per_group + u for u in range(rows_per_group)]
        ss = [scores(rr) for rr in rrs]
        ps = [softmax(s) for s, _ in ss]
        for rr, (_, rs), (p, l) in zip(rrs, ss, ps):
            vw = v_ref[pl.ds(pl.multiple_of(rs * GRID_W, GRID_W), n_keys), :]
            o = jnp.dot(p, vw, preferred_element_type=F32) / l
            out = jnp.where(first, o[:GRID_W], o[GRID_W:])
            o_ref[pl.ds(pl.multiple_of(rr * GRID_W, GRID_W), GRID_W), :] = out.astype(o_ref.dtype)
        return carry

    lax.fori_loop(0, rows_per_step // rows_per_group, group, 0)


def _natten(proj, bias, batch, seq, q_col, k_col, v_col, na_dim):
    t = proj.shape[0]
    rows = seq // GRID_W
    assert rows >= WIN_H and na_dim % LANES == 0 and LANES == 2 * HEAD_DIM
    pairs = na_dim // LANES
    rows_per_step = _pick(rows, (32, 16, 8))
    tq = rows_per_step * GRID_W
    chunks = seq // tq
    kern = functools.partial(_natten_kernel, rows=rows, rows_per_step=rows_per_step,
                             rows_per_group=8)
    return pl.pallas_call(
        kern,
        grid=(batch, pairs, chunks),
        in_specs=[pl.BlockSpec((tq, LANES), lambda b, p, c: (b * chunks + c, q_col + p)),
                  pl.BlockSpec((seq, LANES), lambda b, p, c: (b, k_col + p)),
                  pl.BlockSpec((seq, LANES), lambda b, p, c: (b, v_col + p)),
                  pl.BlockSpec((WIN_H, LANES, WIN_H * GRID_W), lambda b, p, c: (0, p, 0))],
        out_specs=pl.BlockSpec((tq, LANES), lambda b, p, c: (b * chunks + c, p)),
        out_shape=jax.ShapeDtypeStruct((t, na_dim), BF16),
        compiler_params=_params("parallel", "parallel", "arbitrary"),
        name="natten",
    )(proj, proj, proj, bias)


def _out_proj_kernel(yc_ref, yn_ref, x_ref, nag_ref, w_ref, fg_ref, x1_ref, xn_ref, *, c_dim):
    na = _rms(yn_ref[...].astype(F32), nag_ref[...]).astype(BF16)
    acc = jnp.dot(yc_ref[...], w_ref[0:c_dim, :], preferred_element_type=F32)
    acc = acc + jnp.dot(na, w_ref[c_dim:, :], preferred_element_type=F32)
    x1 = x_ref[...] + acc
    x1_ref[...] = x1
    xn_ref[...] = _rms(x1, fg_ref[...]).astype(xn_ref.dtype)


def _out_proj(y_conv, y_na, x2, na_g, w_out, ffn_g):
    t, d = x2.shape
    c_dim, na_dim = y_conv.shape[1], y_na.shape[1]
    tm = _pick(t, (256, 128))
    kern = functools.partial(_out_proj_kernel, c_dim=c_dim)
    return pl.pallas_call(
        kern,
        grid=(t // tm,),
        in_specs=[pl.BlockSpec((tm, c_dim), lambda i: (i, 0)),
                  pl.BlockSpec((tm, na_dim), lambda i: (i, 0)),
                  pl.BlockSpec((tm, d), lambda i: (i, 0)),
                  pl.BlockSpec((1, na_dim), lambda i: (0, 0)),
                  pl.BlockSpec((c_dim + na_dim, d), lambda i: (0, 0)),
                  pl.BlockSpec((1, d), lambda i: (0, 0))],
        out_specs=[pl.BlockSpec((tm, d), lambda i: (i, 0)),
                   pl.BlockSpec((tm, d), lambda i: (i, 0))],
        out_shape=[jax.ShapeDtypeStruct((t, d), F32), jax.ShapeDtypeStruct((t, d), BF16)],
        compiler_params=_params("parallel"),
        name="out_proj",
    )(y_conv, y_na, x2, na_g, w_out, ffn_g)


def _ffn_kernel(prev_ref, main_ref, next_ref, wg_ref, wv_ref, dwg_ref, dwv_ref, bg_ref, bv_ref,
                wd_ref, x1_ref, fng_ref, o_ref, lhs_ref, u_ref, acc_ref,
                *, tm, tiles_per_seq, final_norm):
    i = pl.program_id(0)
    j = pl.program_id(1)

    @pl.when(j == 0)
    def _():
        at_start = (i % tiles_per_seq) == 0
        at_end = (i % tiles_per_seq) == tiles_per_seq - 1
        lhs_ref[0:HALO, :] = jnp.where(at_start, jnp.zeros_like(prev_ref), prev_ref[...])
        lhs_ref[HALO:HALO + tm, :] = main_ref[...]
        lhs_ref[HALO + tm:, :] = jnp.where(at_end, jnp.zeros_like(next_ref), next_ref[...])
        acc_ref[...] = jnp.zeros_like(acc_ref)

    def branch(w_ref, dw_ref, b_ref):
        u_ref[...] = jnp.dot(lhs_ref[...], w_ref[...], preferred_element_type=F32)
        return (dw_ref[0:1, :] * u_ref[pl.ds(HALO - 1, tm), :]
                + dw_ref[1:2, :] * u_ref[pl.ds(HALO, tm), :]
                + dw_ref[2:3, :] * u_ref[pl.ds(HALO + 1, tm), :] + b_ref[...])

    gate = branch(wg_ref, dwg_ref, bg_ref)
    val = branch(wv_ref, dwv_ref, bv_ref)
    h = (gate * jax.nn.sigmoid(gate) * val).astype(BF16)
    acc_ref[...] += jnp.dot(h, wd_ref[...], preferred_element_type=F32)

    @pl.when(j == pl.num_programs(1) - 1)
    def _():
        y = x1_ref[...] + acc_ref[...]
        o_ref[...] = _rms(y, fng_ref[...]) if final_norm else y


def _ffn(xn, x1, w_up, dw_w, dw_b, w_down, fn_g, seq, final_norm):
    t, d = xn.shape
    d_ff = w_down.shape[0]
    assert dw_w.shape[0] == 3
    tm = _pick(seq, (512, 256, 128))
    tf = _pick(d_ff, (512, 256, 128))
    nf = d_ff // tf
    prev, nxt = _halo_specs(tm, d, t)
    kern = functools.partial(_ffn_kernel, tm=tm, tiles_per_seq=seq // tm, final_norm=final_norm)
    return pl.pallas_call(
        kern,
        grid=(t // tm, nf),
        in_specs=[prev, pl.BlockSpec((tm, d), lambda i, j: (i, 0)), nxt,
                  pl.BlockSpec((d, tf), lambda i, j: (0, j)),
                  pl.BlockSpec((d, tf), lambda i, j: (0, j + nf)),
                  pl.BlockSpec((3, tf), lambda i, j: (0, j)),
                  pl.BlockSpec((3, tf), lambda i, j: (0, j + nf)),
                  pl.BlockSpec((1, tf), lambda i, j: (0, j)),
                  pl.BlockSpec((1, tf), lambda i, j: (0, j + nf)),
                  pl.BlockSpec((tf, d), lambda i, j: (j, 0)),
                  pl.BlockSpec((tm, d), lambda i, j: (i, 0)),
                  pl.BlockSpec((1, d), lambda i, j: (0, 0))],
        out_specs=pl.BlockSpec((tm, d), lambda i, j: (i, 0)),
        out_shape=jax.ShapeDtypeStruct((t, d), F32),
        scratch_shapes=[pltpu.VMEM((tm + 2 * HALO, d), BF16),
                        pltpu.VMEM((tm + 2 * HALO, tf), F32),
                        pltpu.VMEM((tm, d), F32)],
        compiler_params=_params("parallel", "arbitrary"),
        name="conv_ffn",
    )(xn, xn, xn, w_up, w_up, dw_w, dw_w, dw_b, dw_b, w_down, x1, fn_g)


def kernel(x, attn_norm_g, w_in, conv_dw_w, conv_dw_b, conv_ln_g, conv_ln_b, rpb, conv_out_g,
           na_out_g, w_out, ffn_norm_g, w_up, ffn_dw_w, ffn_dw_b, w_down, final_norm_g):
    batch, seq, d = x.shape
    depth = w_in.shape[0]
    c_dim = conv_dw_w.shape[-1]
    na_dim = na_out_g.shape[-1]
    assert seq % GRID_W == 0 and w_in.shape[-1] == 2 * c_dim + 3 * na_dim
    q_col = 2 * c_dim // LANES
    k_col = q_col + na_dim // LANES
    v_col = k_col + na_dim // LANES
    row = lambda a: a.reshape(1, -1).astype(F32)

    x2 = x.reshape(batch * seq, d)
    for l in range(depth):
        proj = _in_proj(x2, row(attn_norm_g[l]), w_in[l].astype(BF16))
        y_conv = _conv_branch(proj, conv_dw_w[l], row(conv_dw_b[l]), row(conv_ln_g[l]),
                              row(conv_ln_b[l]), row(conv_out_g[l]), seq)
        y_na = _natten(proj, _bias_table(rpb[l]), batch, seq, q_col, k_col, v_col, na_dim)
        x1, xn = _out_proj(y_conv, y_na, x2, row(na_out_g[l]), w_out[l].astype(BF16),
                           row(ffn_norm_g[l]))
        last = l == depth - 1
        x2 = _ffn(xn, x1, w_up[l].astype(BF16), ffn_dw_w[l], row(ffn_dw_b[l]),
                  w_down[l].astype(BF16), row(final_norm_g), seq, final_norm=last)
    return x2.reshape(batch, seq, d)
```

```python
import functools

import numpy as np
import jax
import jax.numpy as jnp
from jax import lax
from jax.experimental import pallas as pl
from jax.experimental.pallas import tpu as pltpu

GRID_W = 64
WIN_H = 8
WIN_W = 16
HEAD_DIM = 64
EPS = 1e-6
MASKED = -1e30

LANES = 128
SUBLANES = 8
HALO = 16
VMEM_LIMIT = 56 * 1024 * 1024

F32 = jnp.float32
BF16 = jnp.bfloat16


def _pick(n, candidates):
    for c in candidates:
        if n % c == 0:
            return c
    raise ValueError(f"no tile in {candidates} divides {n}")


def _params(*semantics):
    return pltpu.CompilerParams(dimension_semantics=semantics, vmem_limit_bytes=VMEM_LIMIT)


def _rms(x, g):
    return x * lax.rsqrt(jnp.mean(x * x, axis=-1, keepdims=True) + EPS) * g


def _column_blocks(w, tn):
    d, n = w.shape
    return jnp.transpose(w.astype(BF16).reshape(d, n // tn, tn), (1, 0, 2))


def _halo_specs(tm, width, n_rows):
    per = tm // HALO
    last = n_rows // HALO - 1
    prev = pl.BlockSpec((HALO, width), lambda i, *_: (jnp.maximum(i * per - 1, 0), 0))
    nxt = pl.BlockSpec((HALO, width), lambda i, *_: (jnp.minimum((i + 1) * per, last), 0))
    return prev, nxt


def _in_proj_kernel(x_ref, g_ref, w_ref, o_ref, xn_ref):
    @pl.when(pl.program_id(1) == 0)
    def _():
        xn_ref[...] = _rms(x_ref[...], g_ref[...]).astype(BF16)

    o_ref[...] = jnp.dot(xn_ref[...], w_ref[...], preferred_element_type=F32).astype(o_ref.dtype)


def _in_proj(x2, g, w):
    t, d = x2.shape
    n = w.shape[1]
    tm = _pick(t, (1024, 512, 256, 128))
    tn = _pick(n, (1280, 1024, 640, 512, 256, 128))
    return pl.pallas_call(
        _in_proj_kernel,
        grid=(t // tm, n // tn),
        in_specs=[pl.BlockSpec((tm, d), lambda i, j: (i, 0)),
                  pl.BlockSpec((1, d), lambda i, j: (0, 0)),
                  pl.BlockSpec((None, d, tn), lambda i, j: (j, 0, 0))],
        out_specs=pl.BlockSpec((tm, tn), lambda i, j: (i, j)),
        out_shape=jax.ShapeDtypeStruct((t, n), BF16),
        scratch_shapes=[pltpu.VMEM((tm, d), BF16)],
        compiler_params=_params("parallel", "arbitrary"),
        name="in_proj",
    )(x2, g, _column_blocks(w, tn))


def _conv_kernel(prev_ref, main_ref, next_ref, dww_ref, dwb_ref, lng_ref, lnb_ref, og_ref,
                 o_ref, h_ref, sh_ref, c_ref, *, tm, tiles_per_seq, c_dim, taps, row_chunk):
    i = pl.program_id(0)
    at_start = (i % tiles_per_seq) == 0
    at_end = (i % tiles_per_seq) == tiles_per_seq - 1
    pad = taps // 2

    def glu(u):
        u = u.astype(F32)
        return u[:, :c_dim] * jax.nn.sigmoid(u[:, c_dim:])

    h_ref[0:HALO, :] = jnp.where(at_start, 0.0, glu(prev_ref[...]))
    h_ref[HALO:HALO + tm, :] = glu(main_ref[...])
    h_ref[HALO + tm:, :] = jnp.where(at_end, 0.0, glu(next_ref[...]))

    n_sh = sh_ref.shape[1]
    base = HALO - pad
    for lc in range(c_dim // LANES):
        lanes = pl.ds(lc * LANES, LANES)
        for b in range(SUBLANES):
            sh_ref[b] = h_ref[pl.ds(b, n_sh), lanes]

        def rows(rc, carry):
            r0 = pl.multiple_of(rc * row_chunk, row_chunk)
            acc = jnp.broadcast_to(dwb_ref[:, lanes], (row_chunk, LANES))
            for k in range(taps):
                a, b = divmod(base + k, SUBLANES)
                acc = acc + dww_ref[k:k + 1, lanes] * sh_ref[b, pl.ds(r0 + a * SUBLANES, row_chunk), :]
            c_ref[pl.ds(r0, row_chunk), lanes] = acc
            return carry

        lax.fori_loop(0, tm // row_chunk, rows, 0)

    ln_rows = min(tm, 128)

    def norm(rc, carry):
        r0 = pl.multiple_of(rc * ln_rows, ln_rows)
        c = c_ref[pl.ds(r0, ln_rows), :]
        mu = jnp.mean(c, axis=-1, keepdims=True)
        d = c - mu
        var = jnp.mean(d * d, axis=-1, keepdims=True)
        y = d * lax.rsqrt(var + EPS) * lng_ref[...] + lnb_ref[...]
        s = y * jax.nn.sigmoid(y)
        o_ref[pl.ds(r0, ln_rows), :] = _rms(s, og_ref[...]).astype(o_ref.dtype)
        return carry

    lax.fori_loop(0, tm // ln_rows, norm, 0)


def _conv_branch(proj, dww, dwb, lng, lnb, og, seq):
    t = proj.shape[0]
    taps, c_dim = dww.shape
    assert taps // 2 < HALO and c_dim % LANES == 0
    tm = _pick(seq, (512, 256, 128))
    row_chunk = 64
    n_sh = tm + (HALO + taps // 2) // SUBLANES * SUBLANES
    prev, nxt = _halo_specs(tm, 2 * c_dim, t)
    vec = pl.BlockSpec((1, c_dim), lambda i: (0, 0))
    kern = functools.partial(_conv_kernel, tm=tm, tiles_per_seq=seq // tm, c_dim=c_dim,
                             taps=taps, row_chunk=row_chunk)
    return pl.pallas_call(
        kern,
        grid=(t // tm,),
        in_specs=[prev, pl.BlockSpec((tm, 2 * c_dim), lambda i: (i, 0)), nxt,
                  pl.BlockSpec((taps, c_dim), lambda i: (0, 0)), vec, vec, vec, vec],
        out_specs=pl.BlockSpec((tm, c_dim), lambda i: (i, 0)),
        out_shape=jax.ShapeDtypeStruct((t, c_dim), BF16),
        scratch_shapes=[pltpu.VMEM((tm + 2 * HALO, c_dim), F32),
                        pltpu.VMEM((SUBLANES, n_sh, LANES), F32),
                        pltpu.VMEM((tm, c_dim), F32)],
        compiler_params=_params("parallel"),
        name="conv_branch",
    )(proj, proj, proj, dww, dwb, lng, lnb, og)


def _bias_table(rpb):
    n_heads = rpb.shape[0]
    cols = np.arange(GRID_W)
    col_start = np.clip(cols - WIN_W // 2, 0, GRID_W - WIN_W)
    kc = np.arange(GRID_W)
    rel = kc[None, :] - cols[:, None] + (WIN_W - 1)
    valid = (kc[None, :] >= col_start[:, None]) & (kc[None, :] < col_start[:, None] + WIN_W)
    t = rpb.astype(F32)[:, :, np.clip(rel, 0, 2 * WIN_W - 2)]
    t = jnp.where(valid[None, None], t, MASKED)
    tabs = [jnp.transpose(t[:, d:d + WIN_H], (0, 2, 1, 3)).reshape(n_heads * GRID_W, WIN_H * GRID_W)
            for d in range(WIN_H)]
    return jnp.stack(tabs)


def _natten_kernel(q_ref, k_ref, v_ref, bias_ref, o_ref, *, rows, rows_per_step, rows_per_group):
    c = pl.program_id(2)
    lane = lax.broadcasted_iota(jnp.int32, (GRID_W, LANES), 1)
    first = lane < HEAD_DIM
    scale = HEAD_DIM ** -0.5
    n_keys = WIN_H * GRID_W

    def scores(rr):
        r = c * rows_per_step + rr
        rs = jnp.clip(r - WIN_H // 2, 0, rows - WIN_H)
        d = rs - r + (WIN_H - 1)
        q = q_ref[pl.ds(pl.multiple_of(rr * GRID_W, GRID_W), GRID_W), :] * scale
        zero = jnp.zeros_like(q)
        q2 = jnp.concatenate([jnp.where(first, q, zero), jnp.where(first, zero, q)], axis=0)
        kw = k_ref[pl.ds(pl.multiple_of(rs * GRID_W, GRID_W), n_keys), :]
        s = lax.dot_general(q2, kw, (((1,), (1,)), ((), ())), preferred_element_type=F32)
        return s + bias_ref[d], rs

    def softmax(s):
        m = jnp.max(s, axis=-1, keepdims=True)
        p = jnp.exp(s - m)
        return p.astype(BF16), jnp.sum(p, axis=-1, keepdims=True)

    def group(g, carry):
        rrs = [g * rows_per_group + u for u in range(rows_per_group)]
        ss = [scores(rr) for rr in rrs]
        ps = [softmax(s) for s, _ in ss]
        for rr, (_, rs), (p, l) in zip(rrs, ss, ps):
            vw = v_ref[pl.ds(pl.multiple_of(rs * GRID_W, GRID_W), n_keys), :]
            o = jnp.dot(p, vw, preferred_element_type=F32) / l
            out = jnp.where(first, o[:GRID_W], o[GRID_W:])
            o_ref[pl.ds(pl.multiple_of(rr * GRID_W, GRID_W), GRID_W), :] = out.astype(o_ref.dtype)
        return carry

    lax.fori_loop(0, rows_per_step // rows_per_group, group, 0)


def _natten(proj, bias, batch, seq, q_col, k_col, v_col, na_dim):
    t = proj.shape[0]
    rows = seq // GRID_W
    assert rows >= WIN_H and na_dim % LANES == 0 and LANES == 2 * HEAD_DIM
    pairs = na_dim // LANES
    rows_per_step = _pick(rows, (32, 16, 8))
    tq = rows_per_step * GRID_W
    chunks = seq // tq
    kern = functools.partial(_natten_kernel, rows=rows, rows_per_step=rows_per_step,
                             rows_per_group=8)
    return pl.pallas_call(
        kern,
        grid=(batch, pairs, chunks),
        in_specs=[pl.BlockSpec((tq, LANES), lambda b, p, c: (b * chunks + c, q_col + p)),
                  pl.BlockSpec((seq, LANES), lambda b, p, c: (b, k_col + p)),
                  pl.BlockSpec((seq, LANES), lambda b, p, c: (b, v_col + p)),
                  pl.BlockSpec((WIN_H, LANES, WIN_H * GRID_W), lambda b, p, c: (0, p, 0))],
        out_specs=pl.BlockSpec((tq, LANES), lambda b, p, c: (b * chunks + c, p)),
        out_shape=jax.ShapeDtypeStruct((t, na_dim), BF16),
        compiler_params=_params("parallel", "parallel", "arbitrary"),
        name="natten",
    )(proj, proj, proj, bias)


def _out_proj_kernel(yc_ref, yn_ref, x_ref, nag_ref, w_ref, fg_ref, x1_ref, xn_ref, *, c_dim):
    na = _rms(yn_ref[...].astype(F32), nag_ref[...]).astype(BF16)
    acc = jnp.dot(yc_ref[...], w_ref[0:c_dim, :], preferred_element_type=F32)
    acc = acc + jnp.dot(na, w_ref[c_dim:, :], preferred_element_type=F32)
    x1 = x_ref[...] + acc
    x1_ref[...] = x1
    xn_ref[...] = _rms(x1, fg_ref[...]).astype(xn_ref.dtype)


def _out_proj(y_conv, y_na, x2, na_g, w_out, ffn_g):
    t, d = x2.shape
    c_dim, na_dim = y_conv.shape[1], y_na.shape[1]
    tm = _pick(t, (256, 128))
    kern = functools.partial(_out_proj_kernel, c_dim=c_dim)
    return pl.pallas_call(
        kern,
        grid=(t // tm,),
        in_specs=[pl.BlockSpec((tm, c_dim), lambda i: (i, 0)),
                  pl.BlockSpec((tm, na_dim), lambda i: (i, 0)),
                  pl.BlockSpec((tm, d), lambda i: (i, 0)),
                  pl.BlockSpec((1, na_dim), lambda i: (0, 0)),
                  pl.BlockSpec((c_dim + na_dim, d), lambda i: (0, 0)),
                  pl.BlockSpec((1, d), lambda i: (0, 0))],
        out_specs=[pl.BlockSpec((tm, d), lambda i: (i, 0)),
                   pl.BlockSpec((tm, d), lambda i: (i, 0))],
        out_shape=[jax.ShapeDtypeStruct((t, d), F32), jax.ShapeDtypeStruct((t, d), BF16)],
        compiler_params=_params("parallel"),
        name="out_proj",
    )(y_conv, y_na, x2, na_g, w_out, ffn_g)


def _ffn_kernel(prev_ref, main_ref, next_ref, wg_ref, wv_ref, dwg_ref, dwv_ref, bg_ref, bv_ref,
                wd_ref, x1_ref, fng_ref, o_ref, lhs_ref, ug_ref, uv_ref, h_ref, acc_ref, y_ref,
                *, tm, tiles_per_seq, final_norm):
    i = pl.program_id(0)
    j = pl.program_id(1)
    half_rows = tm // 2

    @pl.when(j == 0)
    def _():
        at_start = (i % tiles_per_seq) == 0
        at_end = (i % tiles_per_seq) == tiles_per_seq - 1
        lhs_ref[0:HALO, :] = jnp.where(at_start, jnp.zeros_like(prev_ref), prev_ref[...])
        lhs_ref[HALO:HALO + tm, :] = main_ref[...]
        lhs_ref[HALO + tm:, :] = jnp.where(at_end, jnp.zeros_like(next_ref), next_ref[...])
        acc_ref[...] = jnp.zeros_like(acc_ref)

    def up(w_ref, u_ref):
        u = jnp.dot(lhs_ref[...], w_ref[...], preferred_element_type=F32)
        for lb in range(u_ref.shape[0]):
            u_ref[lb] = u[:, lb * LANES:(lb + 1) * LANES]

    def conv(u_ref, dw_ref, b_ref, lb):
        cols = slice(lb * LANES, (lb + 1) * LANES)
        rows = [u_ref[lb, pl.ds(HALO - 1 + k, half_rows, stride=2), :] for k in range(4)]
        w = [dw_ref[k:k + 1, cols] for k in range(3)]
        even = w[0] * rows[0] + w[1] * rows[1] + w[2] * rows[2] + b_ref[:, cols]
        odd = w[0] * rows[1] + w[1] * rows[2] + w[2] * rows[3] + b_ref[:, cols]
        return jnp.concatenate([even, odd], axis=0)

    up(wg_ref, ug_ref)
    up(wv_ref, uv_ref)
    for lb in range(ug_ref.shape[0]):
        gate = conv(ug_ref, dwg_ref, bg_ref, lb)
        val = conv(uv_ref, dwv_ref, bv_ref, lb)
        h_ref[:, lb * LANES:(lb + 1) * LANES] = (gate * jax.nn.sigmoid(gate) * val).astype(BF16)
    acc_ref[...] += jnp.dot(h_ref[...], wd_ref[...], preferred_element_type=F32)

    @pl.when(j == pl.num_programs(1) - 1)
    def _():
        for lb in range(y_ref.shape[0]):
            cols = slice(lb * LANES, (lb + 1) * LANES)
            y_ref[lb, pl.ds(0, half_rows, stride=2), :] = acc_ref[0:half_rows, cols]
            y_ref[lb, pl.ds(1, half_rows, stride=2), :] = acc_ref[half_rows:, cols]
        y = x1_ref[...] + jnp.concatenate([y_ref[lb] for lb in range(y_ref.shape[0])], axis=1)
        o_ref[...] = _rms(y, fng_ref[...]) if final_norm else y


def _ffn(xn, x1, w_up, dw_w, dw_b, w_down, fn_g, seq, final_norm):
    t, d = xn.shape
    d_ff = w_down.shape[0]
    assert dw_w.shape[0] == 3 and d % LANES == 0
    tm = _pick(seq, (512, 256, 128))
    tf = _pick(d_ff, (512, 256, 128))
    nf = d_ff // tf
    w_up = _column_blocks(w_up, tf)
    prev, nxt = _halo_specs(tm, d, t)
    kern = functools.partial(_ffn_kernel, tm=tm, tiles_per_seq=seq // tm, final_norm=final_norm)
    return pl.pallas_call(
        kern,
        grid=(t // tm, nf),
        in_specs=[prev, pl.BlockSpec((tm, d), lambda i, j: (i, 0)), nxt,
                  pl.BlockSpec((None, d, tf), lambda i, j: (j, 0, 0)),
                  pl.BlockSpec((None, d, tf), lambda i, j: (j + nf, 0, 0)),
                  pl.BlockSpec((3, tf), lambda i, j: (0, j)),
                  pl.BlockSpec((3, tf), lambda i, j: (0, j + nf)),
                  pl.BlockSpec((1, tf), lambda i, j: (0, j)),
                  pl.BlockSpec((1, tf), lambda i, j: (0, j + nf)),
                  pl.BlockSpec((tf, d), lambda i, j: (j, 0)),
                  pl.BlockSpec((tm, d), lambda i, j: (i, 0)),
                  pl.BlockSpec((1, d), lambda i, j: (0, 0))],
        out_specs=pl.BlockSpec((tm, d), lambda i, j: (i, 0)),
        out_shape=jax.ShapeDtypeStruct((t, d), F32),
        scratch_shapes=[pltpu.VMEM((tm + 2 * HALO, d), BF16),
                        pltpu.VMEM((tf // LANES, tm + 2 * HALO, LANES), F32),
                        pltpu.VMEM((tf // LANES, tm + 2 * HALO, LANES), F32),
                        pltpu.VMEM((tm, tf), BF16),
                        pltpu.VMEM((tm, d), F32),
                        pltpu.VMEM((d // LANES, tm, LANES), F32)],
        compiler_params=_params("parallel", "arbitrary"),
        name="conv_ffn",
    )(xn, xn, xn, w_up, w_up, dw_w, dw_w, dw_b, dw_b, w_down, x1, fn_g)


def kernel(x, attn_norm_g, w_in, conv_dw_w, conv_dw_b, conv_ln_g, conv_ln_b, rpb, conv_out_g,
           na_out_g, w_out, ffn_norm_g, w_up, ffn_dw_w, ffn_dw_b, w_down, final_norm_g):
    batch, seq, d = x.shape
    depth = w_in.shape[0]
    c_dim = conv_dw_w.shape[-1]
    na_dim = na_out_g.shape[-1]
    assert seq % GRID_W == 0 and w_in.shape[-1] == 2 * c_dim + 3 * na_dim
    q_col = 2 * c_dim // LANES
    k_col = q_col + na_dim // LANES
    v_col = k_col + na_dim // LANES
    row = lambda a: a.reshape(1, -1).astype(F32)

    x2 = x.reshape(batch * seq, d)
    for l in range(depth):
        proj = _in_proj(x2, row(attn_norm_g[l]), w_in[l])
        y_conv = _conv_branch(proj, conv_dw_w[l], row(conv_dw_b[l]), row(conv_ln_g[l]),
                              row(conv_ln_b[l]), row(conv_out_g[l]), seq)
        y_na = _natten(proj, _bias_table(rpb[l]), batch, seq, q_col, k_col, v_col, na_dim)
        x1, xn = _out_proj(y_conv, y_na, x2, row(na_out_g[l]), w_out[l].astype(BF16),
                           row(ffn_norm_g[l]))
        last = l == depth - 1
        x2 = _ffn(xn, x1, w_up[l], ffn_dw_w[l], row(ffn_dw_b[l]),
                  w_down[l].astype(BF16), row(final_norm_g), seq, final_norm=last)
    return x2.reshape(batch, seq, d)
```

```python
import functools

import numpy as np
import jax
import jax.numpy as jnp
from jax import lax
from jax.experimental import pallas as pl
from jax.experimental.pallas import tpu as pltpu

GRID_W = 64
WIN_H = 8
WIN_W = 16
HEAD_DIM = 64
EPS = 1e-6
MASKED = -1e30

LANES = 128
SUBLANES = 8
HALO = 16
VMEM_LIMIT = 56 * 1024 * 1024

F32 = jnp.float32
BF16 = jnp.bfloat16


def _pick(n, candidates):
    for c in candidates:
        if n % c == 0:
            return c
    raise ValueError(f"no tile in {candidates} divides {n}")


def _params(*semantics):
    return pltpu.CompilerParams(dimension_semantics=semantics, vmem_limit_bytes=VMEM_LIMIT)


def _rms(x, g):
    return x * lax.rsqrt(jnp.mean(x * x, axis=-1, keepdims=True) + EPS) * g


def _column_blocks(w, tn):
    d, n = w.shape
    return jnp.transpose(w.astype(BF16).reshape(d, n // tn, tn), (1, 0, 2))


def _halo_specs(tm, width, n_rows):
    per = tm // HALO
    last = n_rows // HALO - 1
    prev = pl.BlockSpec((HALO, width), lambda i, *_: (jnp.maximum(i * per - 1, 0), 0))
    nxt = pl.BlockSpec((HALO, width), lambda i, *_: (jnp.minimum((i + 1) * per, last), 0))
    return prev, nxt


def _in_proj_kernel(x_ref, g_ref, w_ref, o_ref, xn_ref):
    @pl.when(pl.program_id(1) == 0)
    def _():
        xn_ref[...] = _rms(x_ref[...], g_ref[...]).astype(BF16)

    o_ref[...] = jnp.dot(xn_ref[...], w_ref[...], preferred_element_type=F32).astype(o_ref.dtype)


def _in_proj(x2, g, w):
    t, d = x2.shape
    n = w.shape[1]
    tm = _pick(t, (1024, 512, 256, 128))
    tn = _pick(n, (1280, 1024, 640, 512, 256, 128))
    return pl.pallas_call(
        _in_proj_kernel,
        grid=(t // tm, n // tn),
        in_specs=[pl.BlockSpec((tm, d), lambda i, j: (i, 0)),
                  pl.BlockSpec((1, d), lambda i, j: (0, 0)),
                  pl.BlockSpec((None, d, tn), lambda i, j: (j, 0, 0))],
        out_specs=pl.BlockSpec((tm, tn), lambda i, j: (i, j)),
        out_shape=jax.ShapeDtypeStruct((t, n), BF16),
        scratch_shapes=[pltpu.VMEM((tm, d), BF16)],
        compiler_params=_params("parallel", "arbitrary"),
        name="in_proj",
    )(x2, g, _column_blocks(w, tn))


def _conv_kernel(prev_ref, main_ref, next_ref, dww_ref, dwb_ref, lng_ref, lnb_ref, og_ref,
                 o_ref, h_ref, sh_ref, c_ref, *, tm, tiles_per_seq, c_dim, taps, row_chunk):
    i = pl.program_id(0)
    at_start = (i % tiles_per_seq) == 0
    at_end = (i % tiles_per_seq) == tiles_per_seq - 1
    pad = taps // 2

    def glu(u):
        u = u.astype(F32)
        return u[:, :c_dim] * jax.nn.sigmoid(u[:, c_dim:])

    h_ref[0:HALO, :] = jnp.where(at_start, 0.0, glu(prev_ref[...]))
    h_ref[HALO:HALO + tm, :] = glu(main_ref[...])
    h_ref[HALO + tm:, :] = jnp.where(at_end, 0.0, glu(next_ref[...]))

    n_sh = sh_ref.shape[1]
    base = HALO - pad
    for lc in range(c_dim // LANES):
        lanes = pl.ds(lc * LANES, LANES)
        for b in range(SUBLANES):
            sh_ref[b] = h_ref[pl.ds(b, n_sh), lanes]

        def rows(rc, carry):
            r0 = pl.multiple_of(rc * row_chunk, row_chunk)
            acc = jnp.broadcast_to(dwb_ref[:, lanes], (row_chunk, LANES))
            for k in range(taps):
                a, b = divmod(base + k, SUBLANES)
                acc = acc + dww_ref[k:k + 1, lanes] * sh_ref[b, pl.ds(r0 + a * SUBLANES, row_chunk), :]
            c_ref[pl.ds(r0, row_chunk), lanes] = acc
            return carry

        lax.fori_loop(0, tm // row_chunk, rows, 0)

    ln_rows = min(tm, 128)

    def norm(rc, carry):
        r0 = pl.multiple_of(rc * ln_rows, ln_rows)
        c = c_ref[pl.ds(r0, ln_rows), :]
        mu = jnp.mean(c, axis=-1, keepdims=True)
        d = c - mu
        var = jnp.mean(d * d, axis=-1, keepdims=True)
        y = d * lax.rsqrt(var + EPS) * lng_ref[...] + lnb_ref[...]
        s = y * jax.nn.sigmoid(y)
        o_ref[pl.ds(r0, ln_rows), :] = _rms(s, og_ref[...]).astype(o_ref.dtype)
        return carry

    lax.fori_loop(0, tm // ln_rows, norm, 0)


def _conv_branch(proj, dww, dwb, lng, lnb, og, seq):
    t = proj.shape[0]
    taps, c_dim = dww.shape
    assert taps // 2 < HALO and c_dim % LANES == 0
    tm = _pick(seq, (512, 256, 128))
    row_chunk = 64
    n_sh = tm + (HALO + taps // 2) // SUBLANES * SUBLANES
    prev, nxt = _halo_specs(tm, 2 * c_dim, t)
    vec = pl.BlockSpec((1, c_dim), lambda i: (0, 0))
    kern = functools.partial(_conv_kernel, tm=tm, tiles_per_seq=seq // tm, c_dim=c_dim,
                             taps=taps, row_chunk=row_chunk)
    return pl.pallas_call(
        kern,
        grid=(t // tm,),
        in_specs=[prev, pl.BlockSpec((tm, 2 * c_dim), lambda i: (i, 0)), nxt,
                  pl.BlockSpec((taps, c_dim), lambda i: (0, 0)), vec, vec, vec, vec],
        out_specs=pl.BlockSpec((tm, c_dim), lambda i: (i, 0)),
        out_shape=jax.ShapeDtypeStruct((t, c_dim), BF16),
        scratch_shapes=[pltpu.VMEM((tm + 2 * HALO, c_dim), F32),
                        pltpu.VMEM((SUBLANES, n_sh, LANES), F32),
                        pltpu.VMEM((tm, c_dim), F32)],
        compiler_params=_params("parallel"),
        name="conv_branch",
    )(proj, proj, proj, dww, dwb, lng, lnb, og)


def _bias_table(rpb):
    n_heads = rpb.shape[0]
    cols = np.arange(GRID_W)
    col_start = np.clip(cols - WIN_W // 2, 0, GRID_W - WIN_W)
    kc = np.arange(GRID_W)
    rel = kc[None, :] - cols[:, None] + (WIN_W - 1)
    valid = (kc[None, :] >= col_start[:, None]) & (kc[None, :] < col_start[:, None] + WIN_W)
    t = rpb.astype(F32)[:, :, np.clip(rel, 0, 2 * WIN_W - 2)]
    t = jnp.where(valid[None, None], t, MASKED)
    tabs = [jnp.transpose(t[:, d:d + WIN_H], (0, 2, 1, 3)).reshape(n_heads * GRID_W, WIN_H * GRID_W)
            for d in range(WIN_H)]
    return jnp.stack(tabs)


def _natten_kernel(q_ref, k_ref, v_ref, bias_ref, o_ref, *, rows, rows_per_step, rows_per_group):
    c = pl.program_id(2)
    lane = lax.broadcasted_iota(jnp.int32, (GRID_W, LANES), 1)
    first = lane < HEAD_DIM
    scale = HEAD_DIM ** -0.5
    n_keys = WIN_H * GRID_W

    def scores(rr):
        r = c * rows_per_step + rr
        rs = jnp.clip(r - WIN_H // 2, 0, rows - WIN_H)
        d = rs - r + (WIN_H - 1)
        q = q_ref[pl.ds(pl.multiple_of(rr * GRID_W, GRID_W), GRID_W), :] * scale
        zero = jnp.zeros_like(q)
        q2 = jnp.concatenate([jnp.where(first, q, zero), jnp.where(first, zero, q)], axis=0)
        kw = k_ref[pl.ds(pl.multiple_of(rs * GRID_W, GRID_W), n_keys), :]
        s = lax.dot_general(q2, kw, (((1,), (1,)), ((), ())), preferred_element_type=F32)
        return s + bias_ref[d], rs

    def softmax(s):
        m = jnp.max(s, axis=-1, keepdims=True)
        p = jnp.exp(s - m)
        return p.astype(BF16), jnp.sum(p, axis=-1, keepdims=True)

    def group(g, carry):
        rrs = [g * rows_per_group + u for u in range(rows_per_group)]
        ss = [scores(rr) for rr in rrs]
        ps = [softmax(s) for s, _ in ss]
        for rr, (_, rs), (p, l) in zip(rrs, ss, ps):
            vw = v_ref[pl.ds(pl.multiple_of(rs * GRID_W, GRID_W), n_keys), :]
            o = jnp.dot(p, vw, preferred_element_type=F32) / l
            out = jnp.where(first, o[:GRID_W], o[GRID_W:])
            o_ref[pl.ds(pl.multiple_of(rr * GRID_W, GRID_W), GRID_W), :] = out.astype(o_ref.dtype)
        return carry

    lax.fori_loop(0, rows_per_step // rows_per_group, group, 0)


def _natten(proj, bias, batch, seq, q_col, k_col, v_col, na_dim):
    t = proj.shape[0]
    rows = seq // GRID_W
    assert rows >= WIN_H and na_dim % LANES == 0 and LANES == 2 * HEAD_DIM
    pairs = na_dim // LANES
    rows_per_step = _pick(rows, (32, 16, 8))
    tq = rows_per_step * GRID_W
    chunks = seq // tq
    kern = functools.partial(_natten_kernel, rows=rows, rows_per_step=rows_per_step,
                             rows_per_group=8)
    return pl.pallas_call(
        kern,
        grid=(batch, pairs, chunks),
        in_specs=[pl.BlockSpec((tq, LANES), lambda b, p, c: (b * chunks + c, q_col + p)),
                  pl.BlockSpec((seq, LANES), lambda b, p, c: (b, k_col + p)),
                  pl.BlockSpec((seq, LANES), lambda b, p, c: (b, v_col + p)),
                  pl.BlockSpec((WIN_H, LANES, WIN_H * GRID_W), lambda b, p, c: (0, p, 0))],
        out_specs=pl.BlockSpec((tq, LANES), lambda b, p, c: (b * chunks + c, p)),
        out_shape=jax.ShapeDtypeStruct((t, na_dim), BF16),
        compiler_params=_params("parallel", "parallel", "arbitrary"),
        name="natten",
    )(proj, proj, proj, bias)


def _out_proj_kernel(yc_ref, yn_ref, x_ref, nag_ref, w_ref, fg_ref, x1_ref, xn_ref, *, c_dim):
    na = _rms(yn_ref[...].astype(F32), nag_ref[...]).astype(BF16)
    acc = jnp.dot(yc_ref[...], w_ref[0:c_dim, :], preferred_element_type=F32)
    acc = acc + jnp.dot(na, w_ref[c_dim:, :], preferred_element_type=F32)
    x1 = x_ref[...] + acc
    x1_ref[...] = x1
    xn_ref[...] = _rms(x1, fg_ref[...]).astype(xn_ref.dtype)


def _out_proj(y_conv, y_na, x2, na_g, w_out, ffn_g):
    t, d = x2.shape
    c_dim, na_dim = y_conv.shape[1], y_na.shape[1]
    tm = _pick(t, (256, 128))
    kern = functools.partial(_out_proj_kernel, c_dim=c_dim)
    return pl.pallas_call(
        kern,
        grid=(t // tm,),
        in_specs=[pl.BlockSpec((tm, c_dim), lambda i: (i, 0)),
                  pl.BlockSpec((tm, na_dim), lambda i: (i, 0)),
                  pl.BlockSpec((tm, d), lambda i: (i, 0)),
                  pl.BlockSpec((1, na_dim), lambda i: (0, 0)),
                  pl.BlockSpec((c_dim + na_dim, d), lambda i: (0, 0)),
                  pl.BlockSpec((1, d), lambda i: (0, 0))],
        out_specs=[pl.BlockSpec((tm, d), lambda i: (i, 0)),
                   pl.BlockSpec((tm, d), lambda i: (i, 0))],
        out_shape=[jax.ShapeDtypeStruct((t, d), F32), jax.ShapeDtypeStruct((t, d), BF16)],
        compiler_params=_params("parallel"),
        name="out_proj",
    )(y_conv, y_na, x2, na_g, w_out, ffn_g)


def _ffn_up_kernel(prev_ref, main_ref, next_ref, wg_ref, wv_ref, dwg_ref, dwv_ref, bg_ref, bv_ref,
                   o_ref, lhs_ref, ug_ref, uv_ref, y_ref, *, tm, tiles_per_seq):
    i = pl.program_id(0)
    j = pl.program_id(1)
    half_rows = tm // 2

    @pl.when(j == 0)
    def _():
        at_start = (i % tiles_per_seq) == 0
        at_end = (i % tiles_per_seq) == tiles_per_seq - 1
        lhs_ref[0:HALO, :] = jnp.where(at_start, jnp.zeros_like(prev_ref), prev_ref[...])
        lhs_ref[HALO:HALO + tm, :] = main_ref[...]
        lhs_ref[HALO + tm:, :] = jnp.where(at_end, jnp.zeros_like(next_ref), next_ref[...])

    def up(w_ref, u_ref):
        u = jnp.dot(lhs_ref[...], w_ref[...], preferred_element_type=F32)
        for lb in range(u_ref.shape[0]):
            u_ref[lb] = u[:, lb * LANES:(lb + 1) * LANES]

    def conv(u_ref, dw_ref, b_ref, lb):
        cols = slice(lb * LANES, (lb + 1) * LANES)
        rows = [u_ref[lb, pl.ds(HALO - 1 + k, half_rows, stride=2), :] for k in range(4)]
        w = [dw_ref[k:k + 1, cols] for k in range(3)]
        even = w[0] * rows[0] + w[1] * rows[1] + w[2] * rows[2] + b_ref[:, cols]
        odd = w[0] * rows[1] + w[1] * rows[2] + w[2] * rows[3] + b_ref[:, cols]
        return even, odd

    up(wg_ref, ug_ref)
    up(wv_ref, uv_ref)
    for lb in range(ug_ref.shape[0]):
        gates = conv(ug_ref, dwg_ref, bg_ref, lb)
        vals = conv(uv_ref, dwv_ref, bv_ref, lb)
        for parity, (gate, val) in enumerate(zip(gates, vals)):
            y_ref[lb, pl.ds(parity, half_rows, stride=2), :] = gate * jax.nn.sigmoid(gate) * val
        o_ref[:, lb * LANES:(lb + 1) * LANES] = y_ref[lb].astype(o_ref.dtype)


def _ffn_up(xn, w_up, dw_w, dw_b, seq):
    t, d = xn.shape
    d_ff = w_up.shape[1] // 2
    assert dw_w.shape[0] == 3
    tm = _pick(seq, (1024, 512, 256, 128))
    tf = _pick(d_ff, (512, 256, 128))
    nf = d_ff // tf
    w_up = _column_blocks(w_up, tf)
    prev, nxt = _halo_specs(tm, d, t)
    kern = functools.partial(_ffn_up_kernel, tm=tm, tiles_per_seq=seq // tm)
    return pl.pallas_call(
        kern,
        grid=(t // tm, nf),
        in_specs=[prev, pl.BlockSpec((tm, d), lambda i, j: (i, 0)), nxt,
                  pl.BlockSpec((None, d, tf), lambda i, j: (j, 0, 0)),
                  pl.BlockSpec((None, d, tf), lambda i, j: (j + nf, 0, 0)),
                  pl.BlockSpec((3, tf), lambda i, j: (0, j)),
                  pl.BlockSpec((3, tf), lambda i, j: (0, j + nf)),
                  pl.BlockSpec((1, tf), lambda i, j: (0, j)),
                  pl.BlockSpec((1, tf), lambda i, j: (0, j + nf))],
        out_specs=pl.BlockSpec((tm, tf), lambda i, j: (i, j)),
        out_shape=jax.ShapeDtypeStruct((t, d_ff), BF16),
        scratch_shapes=[pltpu.VMEM((tm + 2 * HALO, d), BF16),
                        pltpu.VMEM((tf // LANES, tm + 2 * HALO, LANES), F32),
                        pltpu.VMEM((tf // LANES, tm + 2 * HALO, LANES), F32),
                        pltpu.VMEM((tf // LANES, tm, LANES), F32)],
        compiler_params=_params("parallel", "arbitrary"),
        name="ffn_up",
    )(xn, xn, xn, w_up, w_up, dw_w, dw_w, dw_b, dw_b)


def _ffn_down_kernel(h_ref, wd_ref, x1_ref, fng_ref, o_ref, *, final_norm):
    k = pl.program_id(1)

    @pl.when(k == 0)
    def _():
        o_ref[...] = jnp.zeros_like(o_ref)

    o_ref[...] += jnp.dot(h_ref[...], wd_ref[...], preferred_element_type=F32)

    @pl.when(k == pl.num_programs(1) - 1)
    def _():
        y = x1_ref[...] + o_ref[...]
        o_ref[...] = _rms(y, fng_ref[...]) if final_norm else y


def _ffn_down(h, w_down, x1, fn_g, final_norm):
    t, d_ff = h.shape
    d = w_down.shape[1]
    tm = _pick(t, (1024, 512, 256, 128))
    tk = _pick(d_ff, (512, 256, 128))
    kern = functools.partial(_ffn_down_kernel, final_norm=final_norm)
    return pl.pallas_call(
        kern,
        grid=(t // tm, d_ff // tk),
        in_specs=[pl.BlockSpec((tm, tk), lambda i, k: (i, k)),
                  pl.BlockSpec((tk, d), lambda i, k: (k, 0)),
                  pl.BlockSpec((tm, d), lambda i, k: (i, 0)),
                  pl.BlockSpec((1, d), lambda i, k: (0, 0))],
        out_specs=pl.BlockSpec((tm, d), lambda i, k: (i, 0)),
        out_shape=jax.ShapeDtypeStruct((t, d), F32),
        compiler_params=_params("parallel", "arbitrary"),
        name="ffn_down",
    )(h, w_down, x1, fn_g)


def kernel(x, attn_norm_g, w_in, conv_dw_w, conv_dw_b, conv_ln_g, conv_ln_b, rpb, conv_out_g,
           na_out_g, w_out, ffn_norm_g, w_up, ffn_dw_w, ffn_dw_b, w_down, final_norm_g):
    batch, seq, d = x.shape
    depth = w_in.shape[0]
    c_dim = conv_dw_w.shape[-1]
    na_dim = na_out_g.shape[-1]
    assert seq % GRID_W == 0 and w_in.shape[-1] == 2 * c_dim + 3 * na_dim
    q_col = 2 * c_dim // LANES
    k_col = q_col + na_dim // LANES
    v_col = k_col + na_dim // LANES
    row = lambda a: a.reshape(1, -1).astype(F32)

    x2 = x.reshape(batch * seq, d)
    for l in range(depth):
        proj = _in_proj(x2, row(attn_norm_g[l]), w_in[l])
        y_conv = _conv_branch(proj, conv_dw_w[l], row(conv_dw_b[l]), row(conv_ln_g[l]),
                              row(conv_ln_b[l]), row(conv_out_g[l]), seq)
        y_na = _natten(proj, _bias_table(rpb[l]), batch, seq, q_col, k_col, v_col, na_dim)
        x1, xn = _out_proj(y_conv, y_na, x2, row(na_out_g[l]), w_out[l].astype(BF16),
                           row(ffn_norm_g[l]))
        last = l == depth - 1
        h = _ffn_up(xn, w_up[l], ffn_dw_w[l], row(ffn_dw_b[l]), seq)
        x2 = _ffn_down(h, w_down[l].astype(BF16), x1, row(final_norm_g), final_norm=last)
    return x2.reshape(batch, seq, d)
```

```python
import functools

import numpy as np
import jax
import jax.numpy as jnp
from jax import lax
from jax.experimental import pallas as pl
from jax.experimental.pallas import tpu as pltpu

GRID_W = 64
WIN_H = 8
WIN_W = 16
HEAD_DIM = 64
EPS = 1e-6
MASKED = -1e30

LANES = 128
SUBLANES = 8
HALO = 16
VMEM_LIMIT = 56 * 1024 * 1024

F32 = jnp.float32
BF16 = jnp.bfloat16


def _pick(n, candidates):
    for c in candidates:
        if n % c == 0:
            return c
    raise ValueError(f"no tile in {candidates} divides {n}")


def _params(*semantics):
    return pltpu.CompilerParams(dimension_semantics=semantics, vmem_limit_bytes=VMEM_LIMIT)


def _rms(x, g):
    return x * lax.rsqrt(jnp.mean(x * x, axis=-1, keepdims=True) + EPS) * g


def _halo_specs(tm, width, n_rows):
    per = tm // HALO
    last = n_rows // HALO - 1
    prev = pl.BlockSpec((HALO, width), lambda i, *_: (jnp.maximum(i * per - 1, 0), 0))
    nxt = pl.BlockSpec((HALO, width), lambda i, *_: (jnp.minimum((i + 1) * per, last), 0))
    return prev, nxt


def _in_proj_kernel(x_ref, g_ref, w_ref, o_ref, xn_ref):
    @pl.when(pl.program_id(1) == 0)
    def _():
        xn_ref[...] = _rms(x_ref[...], g_ref[...]).astype(BF16)

    o_ref[...] = jnp.dot(xn_ref[...], w_ref[...], preferred_element_type=F32).astype(o_ref.dtype)


def _in_proj(x2, g, w):
    t, d = x2.shape
    n = w.shape[1]
    tm = _pick(t, (1024, 512, 256, 128))
    tn = _pick(n, (1280, 1024, 640, 512, 256, 128))
    return pl.pallas_call(
        _in_proj_kernel,
        grid=(t // tm, n // tn),
        in_specs=[pl.BlockSpec((tm, d), lambda i, j: (i, 0)),
                  pl.BlockSpec((1, d), lambda i, j: (0, 0)),
                  pl.BlockSpec((d, tn), lambda i, j: (0, j))],
        out_specs=pl.BlockSpec((tm, tn), lambda i, j: (i, j)),
        out_shape=jax.ShapeDtypeStruct((t, n), BF16),
        scratch_shapes=[pltpu.VMEM((tm, d), BF16)],
        compiler_params=_params("parallel", "arbitrary"),
        name="in_proj",
    )(x2, g, w.astype(BF16))


def _conv_kernel(prev_ref, main_ref, next_ref, dww_ref, dwb_ref, lng_ref, lnb_ref, og_ref,
                 o_ref, h_ref, sh_ref, c_ref, *, tm, tiles_per_seq, c_dim, taps, row_chunk):
    i = pl.program_id(0)
    at_start = (i % tiles_per_seq) == 0
    at_end = (i % tiles_per_seq) == tiles_per_seq - 1
    pad = taps // 2

    def glu(u):
        u = u.astype(F32)
        return u[:, :c_dim] * jax.nn.sigmoid(u[:, c_dim:])

    h_ref[0:HALO, :] = jnp.where(at_start, 0.0, glu(prev_ref[...]))
    h_ref[HALO:HALO + tm, :] = glu(main_ref[...])
    h_ref[HALO + tm:, :] = jnp.where(at_end, 0.0, glu(next_ref[...]))

    n_sh = sh_ref.shape[1]
    base = HALO - pad
    for lc in range(c_dim // LANES):
        lanes = pl.ds(lc * LANES, LANES)
        for b in range(SUBLANES):
            sh_ref[b] = h_ref[pl.ds(b, n_sh), lanes]

        def rows(rc, carry):
            r0 = pl.multiple_of(rc * row_chunk, row_chunk)
            acc = jnp.broadcast_to(dwb_ref[:, lanes], (row_chunk, LANES))
            for k in range(taps):
                a, b = divmod(base + k, SUBLANES)
                acc = acc + dww_ref[k:k + 1, lanes] * sh_ref[b, pl.ds(r0 + a * SUBLANES, row_chunk), :]
            c_ref[pl.ds(r0, row_chunk), lanes] = acc
            return carry

        lax.fori_loop(0, tm // row_chunk, rows, 0)

    ln_rows = min(tm, 128)

    def norm(rc, carry):
        r0 = pl.multiple_of(rc * ln_rows, ln_rows)
        c = c_ref[pl.ds(r0, ln_rows), :]
        mu = jnp.mean(c, axis=-1, keepdims=True)
        d = c - mu
        var = jnp.mean(d * d, axis=-1, keepdims=True)
        y = d * lax.rsqrt(var + EPS) * lng_ref[...] + lnb_ref[...]
        s = y * jax.nn.sigmoid(y)
        o_ref[pl.ds(r0, ln_rows), :] = _rms(s, og_ref[...]).astype(o_ref.dtype)
        return carry

    lax.fori_loop(0, tm // ln_rows, norm, 0)


def _conv_branch(proj, dww, dwb, lng, lnb, og, seq):
    t = proj.shape[0]
    taps, c_dim = dww.shape
    assert taps // 2 < HALO and c_dim % LANES == 0
    tm = _pick(seq, (512, 256, 128))
    row_chunk = 64
    n_sh = tm + (HALO + taps // 2) // SUBLANES * SUBLANES
    prev, nxt = _halo_specs(tm, 2 * c_dim, t)
    vec = pl.BlockSpec((1, c_dim), lambda i: (0, 0))
    kern = functools.partial(_conv_kernel, tm=tm, tiles_per_seq=seq // tm, c_dim=c_dim,
                             taps=taps, row_chunk=row_chunk)
    return pl.pallas_call(
        kern,
        grid=(t // tm,),
        in_specs=[prev, pl.BlockSpec((tm, 2 * c_dim), lambda i: (i, 0)), nxt,
                  pl.BlockSpec((taps, c_dim), lambda i: (0, 0)), vec, vec, vec, vec],
        out_specs=pl.BlockSpec((tm, c_dim), lambda i: (i, 0)),
        out_shape=jax.ShapeDtypeStruct((t, c_dim), BF16),
        scratch_shapes=[pltpu.VMEM((tm + 2 * HALO, c_dim), F32),
                        pltpu.VMEM((SUBLANES, n_sh, LANES), F32),
                        pltpu.VMEM((tm, c_dim), F32)],
        compiler_params=_params("parallel"),
        name="conv_branch",
    )(proj, proj, proj, dww, dwb, lng, lnb, og)


def _bias_table(rpb):
    n_heads = rpb.shape[0]
    cols = np.arange(GRID_W)
    col_start = np.clip(cols - WIN_W // 2, 0, GRID_W - WIN_W)
    kc = np.arange(GRID_W)
    rel = kc[None, :] - cols[:, None] + (WIN_W - 1)
    valid = (kc[None, :] >= col_start[:, None]) & (kc[None, :] < col_start[:, None] + WIN_W)
    t = rpb.astype(F32)[:, :, np.clip(rel, 0, 2 * WIN_W - 2)]
    t = jnp.where(valid[None, None], t, MASKED)
    pairs = jnp.concatenate([t[:, :-1], t[:, 1:]], axis=-1)
    return jnp.transpose(pairs, (1, 0, 2, 3)).reshape(2 * WIN_H - 2, n_heads * GRID_W, 2 * GRID_W)


def _natten_kernel(q_ref, k_ref, v_ref, bias_ref, o_ref, *, rows, rows_per_step, rows_per_group):
    c = pl.program_id(2)
    lane = lax.broadcasted_iota(jnp.int32, (GRID_W, LANES), 1)
    first = lane < HEAD_DIM
    scale = HEAD_DIM ** -0.5
    n_keys = WIN_H * GRID_W

    def scores(rr):
        r = c * rows_per_step + rr
        rs = jnp.clip(r - WIN_H // 2, 0, rows - WIN_H)
        d = rs - r + (WIN_H - 1)
        q = q_ref[pl.ds(pl.multiple_of(rr * GRID_W, GRID_W), GRID_W), :] * scale
        zero = jnp.zeros_like(q)
        q2 = jnp.concatenate([jnp.where(first, q, zero), jnp.where(first, zero, q)], axis=0)
        kw = k_ref[pl.ds(pl.multiple_of(rs * GRID_W, GRID_W), n_keys), :]
        s = lax.dot_general(q2, kw, (((1,), (1,)), ((), ())), preferred_element_type=F32)
        bias = jnp.concatenate([bias_ref[d + 2 * m] for m in range(WIN_H // 2)], axis=1)
        return s + bias, rs

    def softmax(s):
        m = jnp.max(s, axis=-1, keepdims=True)
        p = jnp.exp(s - m)
        return p.astype(BF16), jnp.sum(p, axis=-1, keepdims=True)

    def group(g, carry):
        rrs = [g * rows_per_group + u for u in range(rows_per_group)]
        ss = [scores(rr) for rr in rrs]
        ps = [softmax(s) for s, _ in ss]
        for rr, (_, rs), (p, l) in zip(rrs, ss, ps):
            vw = v_ref[pl.ds(pl.multiple_of(rs * GRID_W, GRID_W), n_keys), :]
            o = jnp.dot(p, vw, preferred_element_type=F32) / l
            out = jnp.where(first, o[:GRID_W], o[GRID_W:])
            o_ref[pl.ds(pl.multiple_of(rr * GRID_W, GRID_W), GRID_W), :] = out.astype(o_ref.dtype)
        return carry

    lax.fori_loop(0, rows_per_step // rows_per_group, group, 0)


def _natten(proj, bias, batch, seq, q_col, k_col, v_col, na_dim):
    t = proj.shape[0]
    rows = seq // GRID_W
    assert rows >= WIN_H and na_dim % LANES == 0 and LANES == 2 * HEAD_DIM == 2 * GRID_W
    pairs = na_dim // LANES
    rows_per_step = _pick(rows, (32, 16, 8))
    tq = rows_per_step * GRID_W
    chunks = seq // tq
    kern = functools.partial(_natten_kernel, rows=rows, rows_per_step=rows_per_step,
                             rows_per_group=8)
    return pl.pallas_call(
        kern,
        grid=(batch, pairs, chunks),
        in_specs=[pl.BlockSpec((tq, LANES), lambda b, p, c: (b * chunks + c, q_col + p)),
                  pl.BlockSpec((seq, LANES), lambda b, p, c: (b, k_col + p)),
                  pl.BlockSpec((seq, LANES), lambda b, p, c: (b, v_col + p)),
                  pl.BlockSpec((2 * WIN_H - 2, LANES, LANES), lambda b, p, c: (0, p, 0))],
        out_specs=pl.BlockSpec((tq, LANES), lambda b, p, c: (b * chunks + c, p)),
        out_shape=jax.ShapeDtypeStruct((t, na_dim), BF16),
        compiler_params=_params("parallel", "parallel", "arbitrary"),
        name="natten",
    )(proj, proj, proj, bias)


def _out_proj_kernel(yc_ref, yn_ref, x_ref, nag_ref, w_ref, fg_ref, x1_ref, xn_ref, *, c_dim):
    na = _rms(yn_ref[...].astype(F32), nag_ref[...]).astype(BF16)
    acc = jnp.dot(yc_ref[...], w_ref[0:c_dim, :], preferred_element_type=F32)
    acc = acc + jnp.dot(na, w_ref[c_dim:, :], preferred_element_type=F32)
    x1 = x_ref[...] + acc
    x1_ref[...] = x1
    xn_ref[...] = _rms(x1, fg_ref[...]).astype(xn_ref.dtype)


def _out_proj(y_conv, y_na, x2, na_g, w_out, ffn_g):
    t, d = x2.shape
    c_dim, na_dim = y_conv.shape[1], y_na.shape[1]
    tm = _pick(t, (512, 256, 128))
    kern = functools.partial(_out_proj_kernel, c_dim=c_dim)
    return pl.pallas_call(
        kern,
        grid=(t // tm,),
        in_specs=[pl.BlockSpec((tm, c_dim), lambda i: (i, 0)),
                  pl.BlockSpec((tm, na_dim), lambda i: (i, 0)),
                  pl.BlockSpec((tm, d), lambda i: (i, 0)),
                  pl.BlockSpec((1, na_dim), lambda i: (0, 0)),
                  pl.BlockSpec((c_dim + na_dim, d), lambda i: (0, 0)),
                  pl.BlockSpec((1, d), lambda i: (0, 0))],
        out_specs=[pl.BlockSpec((tm, d), lambda i: (i, 0)),
                   pl.BlockSpec((tm, d), lambda i: (i, 0))],
        out_shape=[jax.ShapeDtypeStruct((t, d), F32), jax.ShapeDtypeStruct((t, d), BF16)],
        compiler_params=_params("parallel"),
        name="out_proj",
    )(y_conv, y_na, x2, na_g, w_out, ffn_g)


def _ffn_up_kernel(prev_ref, main_ref, next_ref, wg_ref, wv_ref, dwg_ref, dwv_ref, bg_ref, bv_ref,
                   o_ref, lhs_ref, ug_ref, uv_ref, y_ref, *, tm, tiles_per_seq):
    i = pl.program_id(0)
    j = pl.program_id(1)
    half_rows = tm // 2

    @pl.when(j == 0)
    def _():
        at_start = (i % tiles_per_seq) == 0
        at_end = (i % tiles_per_seq) == tiles_per_seq - 1
        lhs_ref[0:HALO, :] = jnp.where(at_start, jnp.zeros_like(prev_ref), prev_ref[...])
        lhs_ref[HALO:HALO + tm, :] = main_ref[...]
        lhs_ref[HALO + tm:, :] = jnp.where(at_end, jnp.zeros_like(next_ref), next_ref[...])

    def up(w_ref, u_ref):
        u = jnp.dot(lhs_ref[...], w_ref[...], preferred_element_type=F32)
        for lb in range(u_ref.shape[0]):
            u_ref[lb] = u[:, lb * LANES:(lb + 1) * LANES]

    def conv(u_ref, dw_ref, b_ref, lb):
        cols = slice(lb * LANES, (lb + 1) * LANES)
        rows = [u_ref[lb, pl.ds(HALO - 1 + k, half_rows, stride=2), :] for k in range(4)]
        w = [dw_ref[k:k + 1, cols] for k in range(3)]
        even = w[0] * rows[0] + w[1] * rows[1] + w[2] * rows[2] + b_ref[:, cols]
        odd = w[0] * rows[1] + w[1] * rows[2] + w[2] * rows[3] + b_ref[:, cols]
        return even, odd

    up(wg_ref, ug_ref)
    up(wv_ref, uv_ref)
    for lb in range(ug_ref.shape[0]):
        gates = conv(ug_ref, dwg_ref, bg_ref, lb)
        vals = conv(uv_ref, dwv_ref, bv_ref, lb)
        for parity, (gate, val) in enumerate(zip(gates, vals)):
            y_ref[lb, pl.ds(parity, half_rows, stride=2), :] = gate * jax.nn.sigmoid(gate) * val
        o_ref[:, lb * LANES:(lb + 1) * LANES] = y_ref[lb].astype(o_ref.dtype)


def _ffn_up(xn, w_up, dw_w, dw_b, seq):
    t, d = xn.shape
    d_ff = w_up.shape[1] // 2
    assert dw_w.shape[0] == 3
    tm = _pick(seq, (1024, 512, 256, 128))
    tf = _pick(d_ff, (512, 256, 128))
    nf = d_ff // tf
    w_up = w_up.astype(BF16)
    prev, nxt = _halo_specs(tm, d, t)
    kern = functools.partial(_ffn_up_kernel, tm=tm, tiles_per_seq=seq // tm)
    return pl.pallas_call(
        kern,
        grid=(t // tm, nf),
        in_specs=[prev, pl.BlockSpec((tm, d), lambda i, j: (i, 0)), nxt,
                  pl.BlockSpec((d, tf), lambda i, j: (0, j)),
                  pl.BlockSpec((d, tf), lambda i, j: (0, j + nf)),
                  pl.BlockSpec((3, tf), lambda i, j: (0, j)),
                  pl.BlockSpec((3, tf), lambda i, j: (0, j + nf)),
                  pl.BlockSpec((1, tf), lambda i, j: (0, j)),
                  pl.BlockSpec((1, tf), lambda i, j: (0, j + nf))],
        out_specs=pl.BlockSpec((tm, tf), lambda i, j: (i, j)),
        out_shape=jax.ShapeDtypeStruct((t, d_ff), BF16),
        scratch_shapes=[pltpu.VMEM((tm + 2 * HALO, d), BF16),
                        pltpu.VMEM((tf // LANES, tm + 2 * HALO, LANES), F32),
                        pltpu.VMEM((tf // LANES, tm + 2 * HALO, LANES), F32),
                        pltpu.VMEM((tf // LANES, tm, LANES), F32)],
        compiler_params=_params("parallel", "arbitrary"),
        name="ffn_up",
    )(xn, xn, xn, w_up, w_up, dw_w, dw_w, dw_b, dw_b)


def _ffn_down_kernel(h_ref, wd_ref, x1_ref, fng_ref, o_ref, *, final_norm):
    k = pl.program_id(1)

    @pl.when(k == 0)
    def _():
        o_ref[...] = x1_ref[...]

    o_ref[...] += jnp.dot(h_ref[...], wd_ref[...], preferred_element_type=F32)

    if final_norm:
        @pl.when(k == pl.num_programs(1) - 1)
        def _():
            o_ref[...] = _rms(o_ref[...], fng_ref[...])


def _ffn_down(h, w_down, x1, fn_g, final_norm):
    t, d_ff = h.shape
    d = w_down.shape[1]
    tm = _pick(t, (1024, 512, 256, 128))
    tk = _pick(d_ff, (512, 256, 128))
    kern = functools.partial(_ffn_down_kernel, final_norm=final_norm)
    return pl.pallas_call(
        kern,
        grid=(t // tm, d_ff // tk),
        in_specs=[pl.BlockSpec((tm, tk), lambda i, k: (i, k)),
                  pl.BlockSpec((tk, d), lambda i, k: (k, 0)),
                  pl.BlockSpec((tm, d), lambda i, k: (i, 0)),
                  pl.BlockSpec((1, d), lambda i, k: (0, 0))],
        out_specs=pl.BlockSpec((tm, d), lambda i, k: (i, 0)),
        out_shape=jax.ShapeDtypeStruct((t, d), F32),
        compiler_params=_params("parallel", "arbitrary"),
        name="ffn_down",
    )(h, w_down, x1, fn_g)


def kernel(x, attn_norm_g, w_in, conv_dw_w, conv_dw_b, conv_ln_g, conv_ln_b, rpb, conv_out_g,
           na_out_g, w_out, ffn_norm_g, w_up, ffn_dw_w, ffn_dw_b, w_down, final_norm_g):
    batch, seq, d = x.shape
    depth = w_in.shape[0]
    c_dim = conv_dw_w.shape[-1]
    na_dim = na_out_g.shape[-1]
    assert seq % GRID_W == 0 and w_in.shape[-1] == 2 * c_dim + 3 * na_dim
    q_col = 2 * c_dim // LANES
    k_col = q_col + na_dim // LANES
    v_col = k_col + na_dim // LANES
    row = lambda a: a.reshape(1, -1).astype(F32)

    x2 = x.reshape(batch * seq, d)
    for l in range(depth):
        proj = _in_proj(x2, row(attn_norm_g[l]), w_in[l])
        y_conv = _conv_branch(proj, conv_dw_w[l], row(conv_dw_b[l]), row(conv_ln_g[l]),
                              row(conv_ln_b[l]), row(conv_out_g[l]), seq)
        y_na = _natten(proj, _bias_table(rpb[l]), batch, seq, q_col, k_col, v_col, na_dim)
        x1, xn = _out_proj(y_conv, y_na, x2, row(na_out_g[l]), w_out[l].astype(BF16),
                           row(ffn_norm_g[l]))
        last = l == depth - 1
        h = _ffn_up(xn, w_up[l], ffn_dw_w[l], row(ffn_dw_b[l]), seq)
        x2 = _ffn_down(h, w_down[l].astype(BF16), x1, row(final_norm_g), final_norm=last)
    return x2.reshape(batch, seq, d)
```

```python
import functools

import numpy as np
import jax
import jax.numpy as jnp
from jax import lax
from jax.experimental import pallas as pl
from jax.experimental.pallas import tpu as pltpu

GRID_W = 64
WIN_H = 8
WIN_W = 16
HEAD_DIM = 64
EPS = 1e-6
MASKED = -1e30

LANES = 128
SUBLANES = 8
HALO = 16
VMEM_LIMIT = 56 * 1024 * 1024

F32 = jnp.float32
BF16 = jnp.bfloat16


def _pick(n, candidates):
    for c in candidates:
        if n % c == 0:
            return c
    raise ValueError(f"no tile in {candidates} divides {n}")


def _params(*semantics):
    return pltpu.CompilerParams(dimension_semantics=semantics, vmem_limit_bytes=VMEM_LIMIT)


def _rms(x, g):
    return x * lax.rsqrt(jnp.mean(x * x, axis=-1, keepdims=True) + EPS) * g


def _halo_specs(tm, width, n_rows):
    per = tm // HALO
    last = n_rows // HALO - 1
    prev = pl.BlockSpec((HALO, width), lambda i, *_: (jnp.maximum(i * per - 1, 0), 0))
    nxt = pl.BlockSpec((HALO, width), lambda i, *_: (jnp.minimum((i + 1) * per, last), 0))
    return prev, nxt


def _in_proj_kernel(x_ref, g_ref, w_ref, o_ref, xn_ref):
    @pl.when(pl.program_id(1) == 0)
    def _():
        xn_ref[...] = _rms(x_ref[...], g_ref[...]).astype(BF16)

    o_ref[...] = jnp.dot(xn_ref[...], w_ref[...], preferred_element_type=F32).astype(o_ref.dtype)


def _in_proj(x2, g, w):
    t, d = x2.shape
    n = w.shape[1]
    tm = _pick(t, (1024, 512, 256, 128))
    tn = _pick(n, (1280, 1024, 640, 512, 256, 128))
    return pl.pallas_call(
        _in_proj_kernel,
        grid=(t // tm, n // tn),
        in_specs=[pl.BlockSpec((tm, d), lambda i, j: (i, 0)),
                  pl.BlockSpec((1, d), lambda i, j: (0, 0)),
                  pl.BlockSpec((d, tn), lambda i, j: (0, j))],
        out_specs=pl.BlockSpec((tm, tn), lambda i, j: (i, j)),
        out_shape=jax.ShapeDtypeStruct((t, n), BF16),
        scratch_shapes=[pltpu.VMEM((tm, d), BF16)],
        compiler_params=_params("parallel", "arbitrary"),
        name="in_proj",
    )(x2, g, w.astype(BF16))


def _conv_kernel(prev_ref, main_ref, next_ref, dww_ref, dwb_ref, lng_ref, lnb_ref, og_ref,
                 o_ref, h_ref, c_ref, *, tm, tiles_per_seq, c_dim, taps, chunk):
    i = pl.program_id(0)
    at_start = (i % tiles_per_seq) == 0
    at_end = (i % tiles_per_seq) == tiles_per_seq - 1
    n_lb = c_dim // LANES
    base = HALO - taps // 2

    def glu(u_ref, lb):
        val = u_ref[:, lb * LANES:(lb + 1) * LANES].astype(F32)
        gate = u_ref[:, c_dim + lb * LANES:c_dim + (lb + 1) * LANES].astype(F32)
        return val * jax.nn.sigmoid(gate)

    for lb in range(n_lb):
        h_ref[lb, 0:HALO, :] = jnp.where(at_start, 0.0, glu(prev_ref, lb))
        h_ref[lb, HALO:HALO + tm, :] = glu(main_ref, lb)
        h_ref[lb, HALO + tm:, :] = jnp.where(at_end, 0.0, glu(next_ref, lb))

    def lane_block(lb, carry):
        w = dww_ref[lb]
        bias = jnp.broadcast_to(dwb_ref[lb], (chunk // 2, LANES))
        for r0 in range(0, tm, chunk):
            even, odd = bias, bias
            for m in range(taps + 1):
                rows = h_ref[lb, pl.ds(r0 + base + m, chunk // 2, stride=2), :]
                if m < taps:
                    even = even + w[m:m + 1] * rows
                if m >= 1:
                    odd = odd + w[m - 1:m] * rows
            c_ref[lb, pl.ds(r0, chunk // 2, stride=2), :] = even
            c_ref[lb, pl.ds(r0 + 1, chunk // 2, stride=2), :] = odd
        return carry

    lax.fori_loop(0, n_lb, lane_block, 0)

    ln_rows = min(tm, 128)

    def norm(rc, carry):
        r0 = pl.multiple_of(rc * ln_rows, ln_rows)
        c = jnp.concatenate([c_ref[lb, pl.ds(r0, ln_rows), :] for lb in range(n_lb)], axis=1)
        mu = jnp.mean(c, axis=-1, keepdims=True)
        d = c - mu
        var = jnp.mean(d * d, axis=-1, keepdims=True)
        y = d * lax.rsqrt(var + EPS) * lng_ref[...] + lnb_ref[...]
        s = y * jax.nn.sigmoid(y)
        o_ref[pl.ds(r0, ln_rows), :] = _rms(s, og_ref[...]).astype(o_ref.dtype)
        return carry

    lax.fori_loop(0, tm // ln_rows, norm, 0)


def _conv_branch(proj, dww, dwb, lng, lnb, og, seq):
    t = proj.shape[0]
    taps, c_dim = dww.shape
    assert taps // 2 < HALO and c_dim % LANES == 0
    n_lb = c_dim // LANES
    tm = _pick(seq, (512, 256, 128))
    prev, nxt = _halo_specs(tm, 2 * c_dim, t)
    vec = pl.BlockSpec((1, c_dim), lambda i: (0, 0))
    kern = functools.partial(_conv_kernel, tm=tm, tiles_per_seq=seq // tm, c_dim=c_dim,
                             taps=taps, chunk=128)
    dww_blocks = jnp.transpose(dww.reshape(taps, n_lb, LANES), (1, 0, 2))
    dwb_blocks = dwb.reshape(n_lb, 1, LANES)
    return pl.pallas_call(
        kern,
        grid=(t // tm,),
        in_specs=[prev, pl.BlockSpec((tm, 2 * c_dim), lambda i: (i, 0)), nxt,
                  pl.BlockSpec((n_lb, taps, LANES), lambda i: (0, 0, 0)),
                  pl.BlockSpec((n_lb, 1, LANES), lambda i: (0, 0, 0)), vec, vec, vec],
        out_specs=pl.BlockSpec((tm, c_dim), lambda i: (i, 0)),
        out_shape=jax.ShapeDtypeStruct((t, c_dim), BF16),
        scratch_shapes=[pltpu.VMEM((n_lb, tm + 2 * HALO, LANES), F32),
                        pltpu.VMEM((n_lb, tm, LANES), F32)],
        compiler_params=_params("parallel"),
        name="conv_branch",
    )(proj, proj, proj, dww_blocks, dwb_blocks, lng, lnb, og)


def _bias_table(rpb):
    n_heads = rpb.shape[0]
    cols = np.arange(GRID_W)
    col_start = np.clip(cols - WIN_W // 2, 0, GRID_W - WIN_W)
    kc = np.arange(GRID_W)
    rel = kc[None, :] - cols[:, None] + (WIN_W - 1)
    valid = (kc[None, :] >= col_start[:, None]) & (kc[None, :] < col_start[:, None] + WIN_W)
    t = rpb.astype(F32)[:, :, np.clip(rel, 0, 2 * WIN_W - 2)]
    t = jnp.where(valid[None, None], t, MASKED)
    pairs = jnp.concatenate([t[:, :-1], t[:, 1:]], axis=-1)
    return jnp.transpose(pairs, (1, 0, 2, 3)).reshape(2 * WIN_H - 2, n_heads * GRID_W, 2 * GRID_W)


def _natten_kernel(q_ref, k_ref, v_ref, bias_ref, o_ref, *, rows, rows_per_step, rows_per_group,
                   lag):
    c = pl.program_id(2)
    lane = lax.broadcasted_iota(jnp.int32, (GRID_W, LANES), 1)
    first = lane < HEAD_DIM
    scale = HEAD_DIM ** -0.5
    n_keys = WIN_H * GRID_W

    def scores(rr):
        r = c * rows_per_step + rr
        rs = jnp.clip(r - WIN_H // 2, 0, rows - WIN_H)
        d = rs - r + (WIN_H - 1)
        q = q_ref[pl.ds(pl.multiple_of(rr * GRID_W, GRID_W), GRID_W), :] * scale
        zero = jnp.zeros_like(q)
        q2 = jnp.concatenate([jnp.where(first, q, zero), jnp.where(first, zero, q)], axis=0)
        kw = k_ref[pl.ds(pl.multiple_of(rs * GRID_W, GRID_W), n_keys), :]
        s = lax.dot_general(q2, kw, (((1,), (1,)), ((), ())), preferred_element_type=F32)
        bias = jnp.concatenate([bias_ref[d + 2 * m] for m in range(WIN_H // 2)], axis=1)
        return s + bias, rs

    def softmax(s):
        m = jnp.max(s, axis=-1, keepdims=True)
        p = jnp.exp(s - m)
        return p.astype(BF16), jnp.sum(p, axis=-1, keepdims=True)

    def attend(rr, rs, p, l):
        vw = v_ref[pl.ds(pl.multiple_of(rs * GRID_W, GRID_W), n_keys), :]
        o = jnp.dot(p, vw, preferred_element_type=F32) / l
        out = jnp.where(first, o[:GRID_W], o[GRID_W:])
        o_ref[pl.ds(pl.multiple_of(rr * GRID_W, GRID_W), GRID_W), :] = out.astype(o_ref.dtype)

    def group(g, carry):
        pending = []
        for u in range(rows_per_group + lag):
            if u < rows_per_group:
                rr = g * rows_per_group + u
                s, rs = scores(rr)
                pending.append((rr, rs) + softmax(s))
            if u >= lag:
                attend(*pending[u - lag])
        return carry

    lax.fori_loop(0, rows_per_step // rows_per_group, group, 0)


def _natten(proj, bias, batch, seq, q_col, k_col, v_col, na_dim):
    t = proj.shape[0]
    rows = seq // GRID_W
    assert rows >= WIN_H and na_dim % LANES == 0 and LANES == 2 * HEAD_DIM == 2 * GRID_W
    pairs = na_dim // LANES
    rows_per_step = _pick(rows, (32, 16, 8))
    tq = rows_per_step * GRID_W
    chunks = seq // tq
    kern = functools.partial(_natten_kernel, rows=rows, rows_per_step=rows_per_step,
                             rows_per_group=32, lag=6)
    return pl.pallas_call(
        kern,
        grid=(batch, pairs, chunks),
        in_specs=[pl.BlockSpec((tq, LANES), lambda b, p, c: (b * chunks + c, q_col + p)),
                  pl.BlockSpec((seq, LANES), lambda b, p, c: (b, k_col + p)),
                  pl.BlockSpec((seq, LANES), lambda b, p, c: (b, v_col + p)),
                  pl.BlockSpec((2 * WIN_H - 2, LANES, LANES), lambda b, p, c: (0, p, 0))],
        out_specs=pl.BlockSpec((tq, LANES), lambda b, p, c: (b * chunks + c, p)),
        out_shape=jax.ShapeDtypeStruct((t, na_dim), BF16),
        compiler_params=_params("parallel", "parallel", "arbitrary"),
        name="natten",
    )(proj, proj, proj, bias)


def _out_proj_kernel(yc_ref, yn_ref, x_ref, nag_ref, w_ref, fg_ref, x1_ref, xn_ref, *, c_dim):
    na = _rms(yn_ref[...].astype(F32), nag_ref[...]).astype(BF16)
    acc = jnp.dot(yc_ref[...], w_ref[0:c_dim, :], preferred_element_type=F32)
    acc = acc + jnp.dot(na, w_ref[c_dim:, :], preferred_element_type=F32)
    x1 = x_ref[...] + acc
    x1_ref[...] = x1
    xn_ref[...] = _rms(x1, fg_ref[...]).astype(xn_ref.dtype)


def _out_proj(y_conv, y_na, x2, na_g, w_out, ffn_g):
    t, d = x2.shape
    c_dim, na_dim = y_conv.shape[1], y_na.shape[1]
    tm = _pick(t, (512, 256, 128))
    kern = functools.partial(_out_proj_kernel, c_dim=c_dim)
    return pl.pallas_call(
        kern,
        grid=(t // tm,),
        in_specs=[pl.BlockSpec((tm, c_dim), lambda i: (i, 0)),
                  pl.BlockSpec((tm, na_dim), lambda i: (i, 0)),
                  pl.BlockSpec((tm, d), lambda i: (i, 0)),
                  pl.BlockSpec((1, na_dim), lambda i: (0, 0)),
                  pl.BlockSpec((c_dim + na_dim, d), lambda i: (0, 0)),
                  pl.BlockSpec((1, d), lambda i: (0, 0))],
        out_specs=[pl.BlockSpec((tm, d), lambda i: (i, 0)),
                   pl.BlockSpec((tm, d), lambda i: (i, 0))],
        out_shape=[jax.ShapeDtypeStruct((t, d), F32), jax.ShapeDtypeStruct((t, d), BF16)],
        compiler_params=_params("parallel"),
        name="out_proj",
    )(y_conv, y_na, x2, na_g, w_out, ffn_g)


def _ffn_up_kernel(prev_ref, main_ref, next_ref, wg_ref, wv_ref, dwg_ref, dwv_ref, bg_ref, bv_ref,
                   o_ref, lhs_ref, ug_ref, uv_ref, y_ref, *, tm, tiles_per_seq):
    i = pl.program_id(0)
    j = pl.program_id(1)

    @pl.when(j == 0)
    def _():
        at_start = (i % tiles_per_seq) == 0
        at_end = (i % tiles_per_seq) == tiles_per_seq - 1
        lhs_ref[0:HALO, :] = jnp.where(at_start, jnp.zeros_like(prev_ref), prev_ref[...])
        lhs_ref[HALO:HALO + tm, :] = main_ref[...]
        lhs_ref[HALO + tm:, :] = jnp.where(at_end, jnp.zeros_like(next_ref), next_ref[...])

    half_rows = tm // 2

    def up(w_ref, u_ref):
        u = jnp.dot(lhs_ref[...], w_ref[...], preferred_element_type=F32)
        for lb in range(u_ref.shape[0]):
            u_ref[lb] = u[:, lb * LANES:(lb + 1) * LANES]

    def conv(u_ref, dw_ref, b_ref, lb):
        cols = slice(lb * LANES, (lb + 1) * LANES)
        rows = [u_ref[lb, pl.ds(HALO - 1 + k, half_rows, stride=2), :] for k in range(4)]
        w = [dw_ref[k:k + 1, cols] for k in range(3)]
        even = w[0] * rows[0] + w[1] * rows[1] + w[2] * rows[2] + b_ref[:, cols]
        odd = w[0] * rows[1] + w[1] * rows[2] + w[2] * rows[3] + b_ref[:, cols]
        return even, odd

    up(wg_ref, ug_ref)
    up(wv_ref, uv_ref)
    for lb in range(ug_ref.shape[0]):
        gates = conv(ug_ref, dwg_ref, bg_ref, lb)
        vals = conv(uv_ref, dwv_ref, bv_ref, lb)
        for parity, (gate, val) in enumerate(zip(gates, vals)):
            y_ref[lb, pl.ds(parity, half_rows, stride=2), :] = gate * jax.nn.sigmoid(gate) * val
        o_ref[:, lb * LANES:(lb + 1) * LANES] = y_ref[lb].astype(o_ref.dtype)


def _ffn_up(xn, w_up, dw_w, dw_b, seq):
    t, d = xn.shape
    d_ff = w_up.shape[1] // 2
    assert dw_w.shape[0] == 3
    tm = _pick(seq, (1024, 512, 256, 128))
    tf = _pick(d_ff, (512, 256, 128))
    nf = d_ff // tf
    w_up = w_up.astype(BF16)
    prev, nxt = _halo_specs(tm, d, t)
    kern = functools.partial(_ffn_up_kernel, tm=tm, tiles_per_seq=seq // tm)
    return pl.pallas_call(
        kern,
        grid=(t // tm, nf),
        in_specs=[prev, pl.BlockSpec((tm, d), lambda i, j: (i, 0)), nxt,
                  pl.BlockSpec((d, tf), lambda i, j: (0, j)),
                  pl.BlockSpec((d, tf), lambda i, j: (0, j + nf)),
                  pl.BlockSpec((3, tf), lambda i, j: (0, j)),
                  pl.BlockSpec((3, tf), lambda i, j: (0, j + nf)),
                  pl.BlockSpec((1, tf), lambda i, j: (0, j)),
                  pl.BlockSpec((1, tf), lambda i, j: (0, j + nf))],
        out_specs=pl.BlockSpec((tm, tf), lambda i, j: (i, j)),
        out_shape=jax.ShapeDtypeStruct((t, d_ff), BF16),
        scratch_shapes=[pltpu.VMEM((tm + 2 * HALO, d), BF16),
                        pltpu.VMEM((tf // LANES, tm + 2 * HALO, LANES), F32),
                        pltpu.VMEM((tf // LANES, tm + 2 * HALO, LANES), F32),
                        pltpu.VMEM((tf // LANES, tm, LANES), F32)],
        compiler_params=_params("parallel", "arbitrary"),
        name="ffn_up",
    )(xn, xn, xn, w_up, w_up, dw_w, dw_w, dw_b, dw_b)


def _ffn_down_kernel(h_ref, wd_ref, x1_ref, fng_ref, o_ref, *, final_norm):
    k = pl.program_id(1)

    @pl.when(k == 0)
    def _():
        o_ref[...] = x1_ref[...]

    o_ref[...] += jnp.dot(h_ref[...], wd_ref[...], preferred_element_type=F32)

    if final_norm:
        @pl.when(k == pl.num_programs(1) - 1)
        def _():
            o_ref[...] = _rms(o_ref[...], fng_ref[...])


def _ffn_down(h, w_down, x1, fn_g, final_norm):
    t, d_ff = h.shape
    d = w_down.shape[1]
    tm = _pick(t, (1024, 512, 256, 128))
    tk = _pick(d_ff, (512, 256, 128))
    kern = functools.partial(_ffn_down_kernel, final_norm=final_norm)
    return pl.pallas_call(
        kern,
        grid=(t // tm, d_ff // tk),
        in_specs=[pl.BlockSpec((tm, tk), lambda i, k: (i, k)),
                  pl.BlockSpec((tk, d), lambda i, k: (k, 0)),
                  pl.BlockSpec((tm, d), lambda i, k: (i, 0)),
                  pl.BlockSpec((1, d), lambda i, k: (0, 0))],
        out_specs=pl.BlockSpec((tm, d), lambda i, k: (i, 0)),
        out_shape=jax.ShapeDtypeStruct((t, d), F32),
        compiler_params=_params("parallel", "arbitrary"),
        name="ffn_down",
    )(h, w_down, x1, fn_g)


def kernel(x, attn_norm_g, w_in, conv_dw_w, conv_dw_b, conv_ln_g, conv_ln_b, rpb, conv_out_g,
           na_out_g, w_out, ffn_norm_g, w_up, ffn_dw_w, ffn_dw_b, w_down, final_norm_g):
    batch, seq, d = x.shape
    depth = w_in.shape[0]
    c_dim = conv_dw_w.shape[-1]
    na_dim = na_out_g.shape[-1]
    assert seq % GRID_W == 0 and w_in.shape[-1] == 2 * c_dim + 3 * na_dim
    q_col = 2 * c_dim // LANES
    k_col = q_col + na_dim // LANES
    v_col = k_col + na_dim // LANES
    row = lambda a: a.reshape(1, -1).astype(F32)

    x2 = x.reshape(batch * seq, d)
    for l in range(depth):
        proj = _in_proj(x2, row(attn_norm_g[l]), w_in[l])
        y_conv = _conv_branch(proj, conv_dw_w[l], row(conv_dw_b[l]), row(conv_ln_g[l]),
                              row(conv_ln_b[l]), row(conv_out_g[l]), seq)
        y_na = _natten(proj, _bias_table(rpb[l]), batch, seq, q_col, k_col, v_col, na_dim)
        x1, xn = _out_proj(y_conv, y_na, x2, row(na_out_g[l]), w_out[l].astype(BF16),
                           row(ffn_norm_g[l]))
        last = l == depth - 1
        h = _ffn_up(xn, w_up[l], ffn_dw_w[l], row(ffn_dw_b[l]), seq)
        x2 = _ffn_down(h, w_down[l].astype(BF16), x1, row(final_norm_g), final_norm=last)
    return x2.reshape(batch, seq, d)
```

```python
import functools

import numpy as np
import jax
import jax.numpy as jnp
from jax import lax
from jax.experimental import pallas as pl
from jax.experimental.pallas import tpu as pltpu

GRID_W = 64
WIN_H = 8
WIN_W = 16
HEAD_DIM = 64
EPS = 1e-6
MASKED = -1e30

LANES = 128
SUBLANES = 8
HALO = 16
VMEM_LIMIT = 56 * 1024 * 1024

F32 = jnp.float32
BF16 = jnp.bfloat16


def _pick(n, candidates):
    for c in candidates:
        if n % c == 0:
            return c
    raise ValueError(f"no tile in {candidates} divides {n}")


def _params(*semantics):
    return pltpu.CompilerParams(dimension_semantics=semantics, vmem_limit_bytes=VMEM_LIMIT)


def _rms(x, g):
    return x * lax.rsqrt(jnp.mean(x * x, axis=-1, keepdims=True) + EPS) * g


def _halo_specs(tm, width, n_rows):
    per = tm // HALO
    last = n_rows // HALO - 1
    prev = pl.BlockSpec((HALO, width), lambda i, *_: (jnp.maximum(i * per - 1, 0), 0))
    nxt = pl.BlockSpec((HALO, width), lambda i, *_: (jnp.minimum((i + 1) * per, last), 0))
    return prev, nxt


def _in_proj_kernel(x_ref, g_ref, w_ref, o_ref):
    xn = _rms(x_ref[...], g_ref[...]).astype(BF16)
    o_ref[...] = jnp.dot(xn, w_ref[...], preferred_element_type=F32).astype(o_ref.dtype)


def _in_proj(x2, g, w):
    t, d = x2.shape
    n = w.shape[1]
    tm = _pick(t, (512, 256, 128))
    return pl.pallas_call(
        _in_proj_kernel,
        grid=(t // tm,),
        in_specs=[pl.BlockSpec((tm, d), lambda i: (i, 0)),
                  pl.BlockSpec((1, d), lambda i: (0, 0)),
                  pl.BlockSpec((d, n), lambda i: (0, 0), pipeline_mode=pl.Buffered(1))],
        out_specs=pl.BlockSpec((tm, n), lambda i: (i, 0)),
        out_shape=jax.ShapeDtypeStruct((t, n), BF16),
        compiler_params=_params("parallel"),
        name="in_proj",
    )(x2, g, w.astype(BF16))


def _conv_kernel(prev_ref, main_ref, next_ref, dww_ref, dwb_ref, lng_ref, lnb_ref, og_ref,
                 o_ref, h_ref, c_ref, *, tm, tiles_per_seq, c_dim, taps, chunk):
    i = pl.program_id(0)
    at_start = (i % tiles_per_seq) == 0
    at_end = (i % tiles_per_seq) == tiles_per_seq - 1
    n_lb = c_dim // LANES
    base = HALO - taps // 2

    def glu(u_ref, lb):
        val = u_ref[:, lb * LANES:(lb + 1) * LANES].astype(F32)
        gate = u_ref[:, c_dim + lb * LANES:c_dim + (lb + 1) * LANES].astype(F32)
        return val * jax.nn.sigmoid(gate)

    for lb in range(n_lb):
        h_ref[lb, 0:HALO, :] = jnp.where(at_start, 0.0, glu(prev_ref, lb))
        h_ref[lb, HALO:HALO + tm, :] = glu(main_ref, lb)
        h_ref[lb, HALO + tm:, :] = jnp.where(at_end, 0.0, glu(next_ref, lb))

    def lane_block(lb, carry):
        w = dww_ref[lb]
        bias = jnp.broadcast_to(dwb_ref[lb], (chunk // 2, LANES))
        for r0 in range(0, tm, chunk):
            even, odd = bias, bias
            for m in range(taps + 1):
                rows = h_ref[lb, pl.ds(r0 + base + m, chunk // 2, stride=2), :]
                if m < taps:
                    even = even + w[m:m + 1] * rows
                if m >= 1:
                    odd = odd + w[m - 1:m] * rows
            c_ref[lb, pl.ds(r0, chunk // 2, stride=2), :] = even
            c_ref[lb, pl.ds(r0 + 1, chunk // 2, stride=2), :] = odd
        return carry

    lax.fori_loop(0, n_lb, lane_block, 0)

    ln_rows = min(tm, 128)

    def norm(rc, carry):
        r0 = pl.multiple_of(rc * ln_rows, ln_rows)
        c = jnp.concatenate([c_ref[lb, pl.ds(r0, ln_rows), :] for lb in range(n_lb)], axis=1)
        mu = jnp.mean(c, axis=-1, keepdims=True)
        d = c - mu
        var = jnp.mean(d * d, axis=-1, keepdims=True)
        y = d * lax.rsqrt(var + EPS) * lng_ref[...] + lnb_ref[...]
        s = y * jax.nn.sigmoid(y)
        o_ref[pl.ds(r0, ln_rows), :] = _rms(s, og_ref[...]).astype(o_ref.dtype)
        return carry

    lax.fori_loop(0, tm // ln_rows, norm, 0)


def _conv_branch(proj, dww, dwb, lng, lnb, og, seq):
    t = proj.shape[0]
    taps, c_dim = dww.shape
    assert taps // 2 < HALO and c_dim % LANES == 0
    n_lb = c_dim // LANES
    tm = _pick(seq, (512, 256, 128))
    prev, nxt = _halo_specs(tm, 2 * c_dim, t)
    vec = pl.BlockSpec((1, c_dim), lambda i: (0, 0))
    kern = functools.partial(_conv_kernel, tm=tm, tiles_per_seq=seq // tm, c_dim=c_dim,
                             taps=taps, chunk=128)
    dww_blocks = jnp.transpose(dww.reshape(taps, n_lb, LANES), (1, 0, 2))
    dwb_blocks = dwb.reshape(n_lb, 1, LANES)
    return pl.pallas_call(
        kern,
        grid=(t // tm,),
        in_specs=[prev, pl.BlockSpec((tm, 2 * c_dim), lambda i: (i, 0)), nxt,
                  pl.BlockSpec((n_lb, taps, LANES), lambda i: (0, 0, 0)),
                  pl.BlockSpec((n_lb, 1, LANES), lambda i: (0, 0, 0)), vec, vec, vec],
        out_specs=pl.BlockSpec((tm, c_dim), lambda i: (i, 0)),
        out_shape=jax.ShapeDtypeStruct((t, c_dim), BF16),
        scratch_shapes=[pltpu.VMEM((n_lb, tm + 2 * HALO, LANES), F32),
                        pltpu.VMEM((n_lb, tm, LANES), F32)],
        compiler_params=_params("parallel"),
        name="conv_branch",
    )(proj, proj, proj, dww_blocks, dwb_blocks, lng, lnb, og)


def _bias_table(rpb):
    n_heads = rpb.shape[0]
    cols = np.arange(GRID_W)
    col_start = np.clip(cols - WIN_W // 2, 0, GRID_W - WIN_W)
    kc = np.arange(GRID_W)
    rel = kc[None, :] - cols[:, None] + (WIN_W - 1)
    valid = (kc[None, :] >= col_start[:, None]) & (kc[None, :] < col_start[:, None] + WIN_W)
    t = rpb.astype(F32)[:, :, np.clip(rel, 0, 2 * WIN_W - 2)]
    t = jnp.where(valid[None, None], t, MASKED)
    pairs = jnp.concatenate([t[:, :-1], t[:, 1:]], axis=-1)
    return jnp.transpose(pairs, (1, 0, 2, 3)).reshape(2 * WIN_H - 2, n_heads * GRID_W, 2 * GRID_W)


def _natten_kernel(q_ref, k_ref, v_ref, bias_ref, o_ref, *, rows, rows_per_step, rows_per_group,
                   lag):
    c = pl.program_id(2)
    lane = lax.broadcasted_iota(jnp.int32, (GRID_W, LANES), 1)
    first = lane < HEAD_DIM
    scale = HEAD_DIM ** -0.5
    n_keys = WIN_H * GRID_W

    def scores(rr):
        r = c * rows_per_step + rr
        rs = jnp.clip(r - WIN_H // 2, 0, rows - WIN_H)
        d = rs - r + (WIN_H - 1)
        q = q_ref[pl.ds(pl.multiple_of(rr * GRID_W, GRID_W), GRID_W), :] * scale
        zero = jnp.zeros_like(q)
        q2 = jnp.concatenate([jnp.where(first, q, zero), jnp.where(first, zero, q)], axis=0)
        kw = k_ref[pl.ds(pl.multiple_of(rs * GRID_W, GRID_W), n_keys), :]
        s = lax.dot_general(q2, kw, (((1,), (1,)), ((), ())), preferred_element_type=F32)
        bias = jnp.concatenate([bias_ref[d + 2 * m] for m in range(WIN_H // 2)], axis=1)
        return s + bias, rs

    def softmax(s):
        m = jnp.max(s, axis=-1, keepdims=True)
        p = jnp.exp(s - m)
        return p.astype(BF16), jnp.sum(p, axis=-1, keepdims=True)

    def attend(rr, rs, p, l):
        vw = v_ref[pl.ds(pl.multiple_of(rs * GRID_W, GRID_W), n_keys), :]
        o = jnp.dot(p, vw, preferred_element_type=F32) / l
        out = jnp.where(first, o[:GRID_W], o[GRID_W:])
        o_ref[pl.ds(pl.multiple_of(rr * GRID_W, GRID_W), GRID_W), :] = out.astype(o_ref.dtype)

    def group(g, carry):
        pending = []
        for u in range(rows_per_group + lag):
            if u < rows_per_group:
                rr = g * rows_per_group + u
                s, rs = scores(rr)
                pending.append((rr, rs) + softmax(s))
            if u >= lag:
                attend(*pending[u - lag])
        return carry

    lax.fori_loop(0, rows_per_step // rows_per_group, group, 0)


def _natten(proj, bias, batch, seq, q_col, k_col, v_col, na_dim):
    t = proj.shape[0]
    rows = seq // GRID_W
    assert rows >= WIN_H and na_dim % LANES == 0 and LANES == 2 * HEAD_DIM == 2 * GRID_W
    pairs = na_dim // LANES
    rows_per_step = _pick(rows, (32, 16, 8))
    tq = rows_per_step * GRID_W
    chunks = seq // tq
    kern = functools.partial(_natten_kernel, rows=rows, rows_per_step=rows_per_step,
                             rows_per_group=32, lag=6)
    return pl.pallas_call(
        kern,
        grid=(batch, pairs, chunks),
        in_specs=[pl.BlockSpec((tq, LANES), lambda b, p, c: (b * chunks + c, q_col + p)),
                  pl.BlockSpec((seq, LANES), lambda b, p, c: (b, k_col + p)),
                  pl.BlockSpec((seq, LANES), lambda b, p, c: (b, v_col + p)),
                  pl.BlockSpec((2 * WIN_H - 2, LANES, LANES), lambda b, p, c: (0, p, 0))],
        out_specs=pl.BlockSpec((tq, LANES), lambda b, p, c: (b * chunks + c, p)),
        out_shape=jax.ShapeDtypeStruct((t, na_dim), BF16),
        compiler_params=_params("parallel", "parallel", "arbitrary"),
        name="natten",
    )(proj, proj, proj, bias)


def _out_proj_kernel(yc_ref, yn_ref, x_ref, nag_ref, w_ref, fg_ref, x1_ref, xn_ref, *, c_dim):
    na = _rms(yn_ref[...].astype(F32), nag_ref[...]).astype(BF16)
    acc = jnp.dot(yc_ref[...], w_ref[0:c_dim, :], preferred_element_type=F32)
    acc = acc + jnp.dot(na, w_ref[c_dim:, :], preferred_element_type=F32)
    x1 = x_ref[...] + acc
    x1_ref[...] = x1
    xn_ref[...] = _rms(x1, fg_ref[...]).astype(xn_ref.dtype)


def _out_proj(y_conv, y_na, x2, na_g, w_out, ffn_g):
    t, d = x2.shape
    c_dim, na_dim = y_conv.shape[1], y_na.shape[1]
    tm = _pick(t, (512, 256, 128))
    kern = functools.partial(_out_proj_kernel, c_dim=c_dim)
    return pl.pallas_call(
        kern,
        grid=(t // tm,),
        in_specs=[pl.BlockSpec((tm, c_dim), lambda i: (i, 0)),
                  pl.BlockSpec((tm, na_dim), lambda i: (i, 0)),
                  pl.BlockSpec((tm, d), lambda i: (i, 0)),
                  pl.BlockSpec((1, na_dim), lambda i: (0, 0)),
                  pl.BlockSpec((c_dim + na_dim, d), lambda i: (0, 0)),
                  pl.BlockSpec((1, d), lambda i: (0, 0))],
        out_specs=[pl.BlockSpec((tm, d), lambda i: (i, 0)),
                   pl.BlockSpec((tm, d), lambda i: (i, 0))],
        out_shape=[jax.ShapeDtypeStruct((t, d), F32), jax.ShapeDtypeStruct((t, d), BF16)],
        compiler_params=_params("parallel"),
        name="out_proj",
    )(y_conv, y_na, x2, na_g, w_out, ffn_g)


def _ffn_up_kernel(prev_ref, main_ref, next_ref, wg0_ref, wg_ref, wv_ref, dwg_ref, dwv_ref,
                   bg_ref, bv_ref, o_ref, lhs_ref, ug_ref, uv_ref, y_ref,
                   *, tm, nf, n_tiles, tiles_per_seq):
    g = pl.program_id(0)
    i = g // nf
    j = g % nf
    slot = i % 2
    half_rows = tm // 2

    def stage(tile, dst):
        at_start = (tile % tiles_per_seq) == 0
        at_end = (tile % tiles_per_seq) == tiles_per_seq - 1
        lhs_ref[dst, 0:HALO, :] = jnp.where(at_start, jnp.zeros_like(prev_ref), prev_ref[...])
        lhs_ref[dst, HALO:HALO + tm, :] = main_ref[...]
        lhs_ref[dst, HALO + tm:, :] = jnp.where(at_end, jnp.zeros_like(next_ref), next_ref[...])

    def up(src, w_ref, u_ref):
        u = jnp.dot(lhs_ref[src], w_ref[...], preferred_element_type=F32)
        for lb in range(u_ref.shape[0]):
            u_ref[lb] = u[:, lb * LANES:(lb + 1) * LANES]

    def conv(u_ref, dw_ref, b_ref, lb):
        cols = slice(lb * LANES, (lb + 1) * LANES)
        rows = [u_ref[lb, pl.ds(HALO - 1 + k, half_rows, stride=2), :] for k in range(4)]
        w = [dw_ref[k:k + 1, cols] for k in range(3)]
        even = w[0] * rows[0] + w[1] * rows[1] + w[2] * rows[2] + b_ref[:, cols]
        odd = w[0] * rows[1] + w[1] * rows[2] + w[2] * rows[3] + b_ref[:, cols]
        return even, odd

    @pl.when(g == 0)
    def _():
        stage(0, 0)
        up(0, wg0_ref, ug_ref)

    @pl.when(j == nf - 2)
    def _():
        stage(jnp.minimum(i + 1, n_tiles - 1), 1 - slot)

    up(slot, wv_ref, uv_ref)
    for lb in range(ug_ref.shape[0]):
        gates = conv(ug_ref, dwg_ref, bg_ref, lb)
        vals = conv(uv_ref, dwv_ref, bv_ref, lb)
        for parity, (gate, val) in enumerate(zip(gates, vals)):
            y_ref[lb, pl.ds(parity, half_rows, stride=2), :] = gate * jax.nn.sigmoid(gate) * val
        o_ref[:, lb * LANES:(lb + 1) * LANES] = y_ref[lb].astype(o_ref.dtype)
    up(jnp.where(j == nf - 1, 1 - slot, slot), wg_ref, ug_ref)


def _ffn_up(xn, w_up, dw_w, dw_b, seq):
    t, d = xn.shape
    d_ff = w_up.shape[1] // 2
    tm = _pick(seq, (1024, 512, 256, 128))
    tf = _pick(d_ff, (512, 256, 128))
    nf = d_ff // tf
    n_tiles = t // tm
    assert dw_w.shape[0] == 3 and nf >= 3
    w_up = w_up.astype(BF16)
    per = tm // HALO
    last_halo = t // HALO - 1
    staged = lambda g: jnp.minimum((g + 2) // nf, n_tiles - 1)
    kern = functools.partial(_ffn_up_kernel, tm=tm, nf=nf, n_tiles=n_tiles,
                             tiles_per_seq=seq // tm)
    return pl.pallas_call(
        kern,
        grid=(n_tiles * nf,),
        in_specs=[pl.BlockSpec((HALO, d), lambda g: (jnp.maximum(staged(g) * per - 1, 0), 0)),
                  pl.BlockSpec((tm, d), lambda g: (staged(g), 0)),
                  pl.BlockSpec((HALO, d), lambda g: (jnp.minimum((staged(g) + 1) * per, last_halo), 0)),
                  pl.BlockSpec((d, tf), lambda g: (0, 0)),
                  pl.BlockSpec((d, tf), lambda g: (0, (g + 1) % nf)),
                  pl.BlockSpec((d, tf), lambda g: (0, g % nf + nf)),
                  pl.BlockSpec((3, tf), lambda g: (0, g % nf)),
                  pl.BlockSpec((3, tf), lambda g: (0, g % nf + nf)),
                  pl.BlockSpec((1, tf), lambda g: (0, g % nf)),
                  pl.BlockSpec((1, tf), lambda g: (0, g % nf + nf))],
        out_specs=pl.BlockSpec((tm, tf), lambda g: (g // nf, g % nf)),
        out_shape=jax.ShapeDtypeStruct((t, d_ff), BF16),
        scratch_shapes=[pltpu.VMEM((2, tm + 2 * HALO, d), BF16),
                        pltpu.VMEM((tf // LANES, tm + 2 * HALO, LANES), F32),
                        pltpu.VMEM((tf // LANES, tm + 2 * HALO, LANES), F32),
                        pltpu.VMEM((tf // LANES, tm, LANES), F32)],
        compiler_params=_params("arbitrary"),
        name="ffn_up",
    )(xn, xn, xn, w_up, w_up, w_up, dw_w, dw_w, dw_b, dw_b)


def _ffn_down_kernel(h_ref, wd_ref, x1_ref, fng_ref, o_ref, *, final_norm):
    y = x1_ref[...] + jnp.dot(h_ref[...], wd_ref[...], preferred_element_type=F32)
    o_ref[...] = _rms(y, fng_ref[...]) if final_norm else y


def _ffn_down(h, w_down, x1, fn_g, final_norm):
    t, d_ff = h.shape
    d = w_down.shape[1]
    tm = _pick(t, (512, 256, 128))
    kern = functools.partial(_ffn_down_kernel, final_norm=final_norm)
    return pl.pallas_call(
        kern,
        grid=(t // tm,),
        in_specs=[pl.BlockSpec((tm, d_ff), lambda i: (i, 0)),
                  pl.BlockSpec((d_ff, d), lambda i: (0, 0), pipeline_mode=pl.Buffered(1)),
                  pl.BlockSpec((tm, d), lambda i: (i, 0)),
                  pl.BlockSpec((1, d), lambda i: (0, 0))],
        out_specs=pl.BlockSpec((tm, d), lambda i: (i, 0)),
        out_shape=jax.ShapeDtypeStruct((t, d), F32),
        compiler_params=_params("parallel"),
        name="ffn_down",
    )(h, w_down, x1, fn_g)


def kernel(x, attn_norm_g, w_in, conv_dw_w, conv_dw_b, conv_ln_g, conv_ln_b, rpb, conv_out_g,
           na_out_g, w_out, ffn_norm_g, w_up, ffn_dw_w, ffn_dw_b, w_down, final_norm_g):
    batch, seq, d = x.shape
    depth = w_in.shape[0]
    c_dim = conv_dw_w.shape[-1]
    na_dim = na_out_g.shape[-1]
    assert seq % GRID_W == 0 and w_in.shape[-1] == 2 * c_dim + 3 * na_dim
    q_col = 2 * c_dim // LANES
    k_col = q_col + na_dim // LANES
    v_col = k_col + na_dim // LANES
    row = lambda a: a.reshape(1, -1).astype(F32)

    x2 = x.reshape(batch * seq, d)
    for l in range(depth):
        proj = _in_proj(x2, row(attn_norm_g[l]), w_in[l])
        y_conv = _conv_branch(proj, conv_dw_w[l], row(conv_dw_b[l]), row(conv_ln_g[l]),
                              row(conv_ln_b[l]), row(conv_out_g[l]), seq)
        y_na = _natten(proj, _bias_table(rpb[l]), batch, seq, q_col, k_col, v_col, na_dim)
        x1, xn = _out_proj(y_conv, y_na, x2, row(na_out_g[l]), w_out[l].astype(BF16),
                           row(ffn_norm_g[l]))
        last = l == depth - 1
        h = _ffn_up(xn, w_up[l], ffn_dw_w[l], row(ffn_dw_b[l]), seq)
        x2 = _ffn_down(h, w_down[l].astype(BF16), x1, row(final_norm_g), final_norm=last)
    return x2.reshape(batch, seq, d)
```

```python
import functools

import numpy as np
import jax
import jax.numpy as jnp
from jax import lax
from jax.experimental import pallas as pl
from jax.experimental.pallas import tpu as pltpu

GRID_W = 64
WIN_H = 8
WIN_W = 16
HEAD_DIM = 64
EPS = 1e-6
MASKED = -1e30
LOG2E = 1.4426950408889634

LANES = 128
SUBLANES = 8
HALO = 16
VMEM_LIMIT = 56 * 1024 * 1024

F32 = jnp.float32
BF16 = jnp.bfloat16


def _pick(n, candidates):
    for c in candidates:
        if n % c == 0:
            return c
    raise ValueError(f"no tile in {candidates} divides {n}")


def _params(*semantics):
    return pltpu.CompilerParams(dimension_semantics=semantics, vmem_limit_bytes=VMEM_LIMIT)


def _rms(x, g):
    return x * lax.rsqrt(jnp.mean(x * x, axis=-1, keepdims=True) + EPS) * g


def _halo_specs(tm, width, n_rows):
    per = tm // HALO
    last = n_rows // HALO - 1
    prev = pl.BlockSpec((HALO, width), lambda i, *_: (jnp.maximum(i * per - 1, 0), 0))
    nxt = pl.BlockSpec((HALO, width), lambda i, *_: (jnp.minimum((i + 1) * per, last), 0))
    return prev, nxt


def _in_proj_kernel(x_ref, g_ref, w_ref, o_ref):
    xn = _rms(x_ref[...], g_ref[...]).astype(BF16)
    o_ref[...] = jnp.dot(xn, w_ref[...], preferred_element_type=F32).astype(o_ref.dtype)


def _in_proj(x2, g, w):
    t, d = x2.shape
    n = w.shape[1]
    tm = _pick(t, (512, 256, 128))
    return pl.pallas_call(
        _in_proj_kernel,
        grid=(t // tm,),
        in_specs=[pl.BlockSpec((tm, d), lambda i: (i, 0)),
                  pl.BlockSpec((1, d), lambda i: (0, 0)),
                  pl.BlockSpec((d, n), lambda i: (0, 0), pipeline_mode=pl.Buffered(1))],
        out_specs=pl.BlockSpec((tm, n), lambda i: (i, 0)),
        out_shape=jax.ShapeDtypeStruct((t, n), BF16),
        compiler_params=_params("parallel"),
        name="in_proj",
    )(x2, g, w.astype(BF16))


def _conv_kernel(prev_ref, main_ref, next_ref, dww_ref, dwb_ref, lng_ref, lnb_ref, og_ref,
                 o_ref, h_ref, c_ref, *, tm, tiles_per_seq, c_dim, taps, chunk):
    i = pl.program_id(0)
    at_start = (i % tiles_per_seq) == 0
    at_end = (i % tiles_per_seq) == tiles_per_seq - 1
    n_lb = c_dim // LANES
    base = HALO - taps // 2

    def glu(u_ref, lb):
        val = u_ref[:, lb * LANES:(lb + 1) * LANES].astype(F32)
        gate = u_ref[:, c_dim + lb * LANES:c_dim + (lb + 1) * LANES].astype(F32)
        return val * jax.nn.sigmoid(gate)

    for lb in range(n_lb):
        h_ref[lb, 0:HALO, :] = jnp.where(at_start, 0.0, glu(prev_ref, lb))
        h_ref[lb, HALO:HALO + tm, :] = glu(main_ref, lb)
        h_ref[lb, HALO + tm:, :] = jnp.where(at_end, 0.0, glu(next_ref, lb))

    def lane_block(lb, carry):
        w = dww_ref[lb]
        bias = jnp.broadcast_to(dwb_ref[lb], (chunk // 2, LANES))
        for r0 in range(0, tm, chunk):
            even, odd = bias, bias
            for m in range(taps + 1):
                rows = h_ref[lb, pl.ds(r0 + base + m, chunk // 2, stride=2), :]
                if m < taps:
                    even = even + w[m:m + 1] * rows
                if m >= 1:
                    odd = odd + w[m - 1:m] * rows
            c_ref[lb, pl.ds(r0, chunk // 2, stride=2), :] = even
            c_ref[lb, pl.ds(r0 + 1, chunk // 2, stride=2), :] = odd
        return carry

    lax.fori_loop(0, n_lb, lane_block, 0)

    ln_rows = min(tm, 128)

    def norm(rc, carry):
        r0 = pl.multiple_of(rc * ln_rows, ln_rows)
        c = jnp.concatenate([c_ref[lb, pl.ds(r0, ln_rows), :] for lb in range(n_lb)], axis=1)
        mu = jnp.mean(c, axis=-1, keepdims=True)
        d = c - mu
        var = jnp.mean(d * d, axis=-1, keepdims=True)
        y = d * lax.rsqrt(var + EPS) * lng_ref[...] + lnb_ref[...]
        s = y * jax.nn.sigmoid(y)
        o_ref[pl.ds(r0, ln_rows), :] = _rms(s, og_ref[...]).astype(o_ref.dtype)
        return carry

    lax.fori_loop(0, tm // ln_rows, norm, 0)


def _conv_branch(proj, dww, dwb, lng, lnb, og, seq):
    t = proj.shape[0]
    taps, c_dim = dww.shape
    assert taps // 2 < HALO and c_dim % LANES == 0
    n_lb = c_dim // LANES
    tm = _pick(seq, (512, 256, 128))
    prev, nxt = _halo_specs(tm, 2 * c_dim, t)
    vec = pl.BlockSpec((1, c_dim), lambda i: (0, 0))
    kern = functools.partial(_conv_kernel, tm=tm, tiles_per_seq=seq // tm, c_dim=c_dim,
                             taps=taps, chunk=128)
    dww_blocks = jnp.transpose(dww.reshape(taps, n_lb, LANES), (1, 0, 2))
    dwb_blocks = dwb.reshape(n_lb, 1, LANES)
    return pl.pallas_call(
        kern,
        grid=(t // tm,),
        in_specs=[prev, pl.BlockSpec((tm, 2 * c_dim), lambda i: (i, 0)), nxt,
                  pl.BlockSpec((n_lb, taps, LANES), lambda i: (0, 0, 0)),
                  pl.BlockSpec((n_lb, 1, LANES), lambda i: (0, 0, 0)), vec, vec, vec],
        out_specs=pl.BlockSpec((tm, c_dim), lambda i: (i, 0)),
        out_shape=jax.ShapeDtypeStruct((t, c_dim), BF16),
        scratch_shapes=[pltpu.VMEM((n_lb, tm + 2 * HALO, LANES), F32),
                        pltpu.VMEM((n_lb, tm, LANES), F32)],
        compiler_params=_params("parallel"),
        name="conv_branch",
    )(proj, proj, proj, dww_blocks, dwb_blocks, lng, lnb, og)


def _bias_table(rpb):
    n_heads = rpb.shape[0]
    cols = np.arange(GRID_W)
    col_start = np.clip(cols - WIN_W // 2, 0, GRID_W - WIN_W)
    kc = np.arange(GRID_W)
    rel = kc[None, :] - cols[:, None] + (WIN_W - 1)
    valid = (kc[None, :] >= col_start[:, None]) & (kc[None, :] < col_start[:, None] + WIN_W)
    lo = GRID_W - WIN_W
    a = jnp.pad(rpb.astype(F32) * LOG2E, ((0, 0), (0, 0), (lo, 2 * GRID_W - lo - (2 * WIN_W - 1))))
    t = jnp.tile(a, (1, 1, GRID_W))[..., :GRID_W * (2 * GRID_W - 1)]
    t = t.reshape(n_heads, -1, GRID_W, 2 * GRID_W - 1)[..., GRID_W - 1:]
    t = jnp.where(valid[None, None], t, MASKED)
    pairs = jnp.concatenate([t[:, :-1], t[:, 1:]], axis=-1)
    return jnp.transpose(pairs, (1, 0, 2, 3)).reshape(2 * WIN_H - 2, n_heads * GRID_W, 2 * GRID_W)


def _natten_kernel(q_ref, k_ref, v_ref, bias_ref, o_ref, *, rows, rows_per_step, rows_per_group,
                   lag):
    c = pl.program_id(2)
    lane = lax.broadcasted_iota(jnp.int32, (GRID_W, LANES), 1)
    first = lane < HEAD_DIM
    n_keys = WIN_H * GRID_W

    def scores(rr):
        r = c * rows_per_step + rr
        rs = jnp.clip(r - WIN_H // 2, 0, rows - WIN_H)
        d = rs - r + (WIN_H - 1)
        q = q_ref[pl.ds(pl.multiple_of(rr * GRID_W, GRID_W), GRID_W), :]
        zero = jnp.zeros_like(q)
        q2 = jnp.concatenate([jnp.where(first, q, zero), jnp.where(first, zero, q)], axis=0)
        kw = k_ref[pl.ds(pl.multiple_of(rs * GRID_W, GRID_W), n_keys), :]
        s = lax.dot_general(q2, kw, (((1,), (1,)), ((), ())), preferred_element_type=F32)
        bias = jnp.concatenate([bias_ref[d + 2 * m] for m in range(WIN_H // 2)], axis=1)
        return s + bias, rs

    def softmax(s):
        m = jnp.max(s, axis=-1, keepdims=True)
        p = jnp.exp2(s - m)
        return p.astype(BF16), jnp.sum(p, axis=-1, keepdims=True)

    def attend(rr, rs, p, l):
        vw = v_ref[pl.ds(pl.multiple_of(rs * GRID_W, GRID_W), n_keys), :]
        o = jnp.dot(p, vw, preferred_element_type=F32) / l
        out = jnp.where(first, o[:GRID_W], o[GRID_W:])
        o_ref[pl.ds(pl.multiple_of(rr * GRID_W, GRID_W), GRID_W), :] = out.astype(o_ref.dtype)

    def group(g, carry):
        pending = []
        for u in range(rows_per_group + lag):
            if u < rows_per_group:
                rr = g * rows_per_group + u
                s, rs = scores(rr)
                pending.append((rr, rs) + softmax(s))
            if u >= lag:
                attend(*pending[u - lag])
        return carry

    lax.fori_loop(0, rows_per_step // rows_per_group, group, 0)


def _natten(proj, bias, batch, seq, q_col, k_col, v_col, na_dim):
    t = proj.shape[0]
    rows = seq // GRID_W
    assert rows >= WIN_H and na_dim % LANES == 0 and LANES == 2 * HEAD_DIM == 2 * GRID_W
    pairs = na_dim // LANES
    rows_per_step = _pick(rows, (32, 16, 8))
    tq = rows_per_step * GRID_W
    chunks = seq // tq
    kern = functools.partial(_natten_kernel, rows=rows, rows_per_step=rows_per_step,
                             rows_per_group=32, lag=3)
    return pl.pallas_call(
        kern,
        grid=(batch, pairs, chunks),
        in_specs=[pl.BlockSpec((tq, LANES), lambda b, p, c: (b * chunks + c, q_col + p)),
                  pl.BlockSpec((seq, LANES), lambda b, p, c: (b, k_col + p)),
                  pl.BlockSpec((seq, LANES), lambda b, p, c: (b, v_col + p)),
                  pl.BlockSpec((2 * WIN_H - 2, LANES, LANES), lambda b, p, c: (0, p, 0))],
        out_specs=pl.BlockSpec((tq, LANES), lambda b, p, c: (b * chunks + c, p)),
        out_shape=jax.ShapeDtypeStruct((t, na_dim), BF16),
        compiler_params=_params("parallel", "parallel", "arbitrary"),
        name="natten",
    )(proj, proj, proj, bias)


def _out_proj_kernel(yc_ref, yn_ref, x_ref, nag_ref, w_ref, fg_ref, x1_ref, xn_ref, *, c_dim):
    na = _rms(yn_ref[...].astype(F32), nag_ref[...]).astype(BF16)
    acc = jnp.dot(yc_ref[...], w_ref[0:c_dim, :], preferred_element_type=F32)
    acc = acc + jnp.dot(na, w_ref[c_dim:, :], preferred_element_type=F32)
    x1 = x_ref[...] + acc
    x1_ref[...] = x1
    xn_ref[...] = _rms(x1, fg_ref[...]).astype(xn_ref.dtype)


def _out_proj(y_conv, y_na, x2, na_g, w_out, ffn_g):
    t, d = x2.shape
    c_dim, na_dim = y_conv.shape[1], y_na.shape[1]
    tm = _pick(t, (512, 256, 128))
    kern = functools.partial(_out_proj_kernel, c_dim=c_dim)
    return pl.pallas_call(
        kern,
        grid=(t // tm,),
        in_specs=[pl.BlockSpec((tm, c_dim), lambda i: (i, 0)),
                  pl.BlockSpec((tm, na_dim), lambda i: (i, 0)),
                  pl.BlockSpec((tm, d), lambda i: (i, 0)),
                  pl.BlockSpec((1, na_dim), lambda i: (0, 0)),
                  pl.BlockSpec((c_dim + na_dim, d), lambda i: (0, 0)),
                  pl.BlockSpec((1, d), lambda i: (0, 0))],
        out_specs=[pl.BlockSpec((tm, d), lambda i: (i, 0)),
                   pl.BlockSpec((tm, d), lambda i: (i, 0))],
        out_shape=[jax.ShapeDtypeStruct((t, d), F32), jax.ShapeDtypeStruct((t, d), BF16)],
        compiler_params=_params("parallel"),
        name="out_proj",
    )(y_conv, y_na, x2, na_g, w_out, ffn_g)


def _ffn_up_kernel(prev_ref, main_ref, next_ref, wg0_ref, wg_ref, wv_ref, dwg_ref, dwv_ref,
                   bg_ref, bv_ref, o_ref, lhs_ref, ug_ref, uv_ref, y_ref,
                   *, tm, nf, n_tiles, tiles_per_seq):
    g = pl.program_id(0)
    i = g // nf
    j = g % nf
    slot = i % 2
    half_rows = tm // 2

    def stage(tile, dst):
        at_start = (tile % tiles_per_seq) == 0
        at_end = (tile % tiles_per_seq) == tiles_per_seq - 1
        lhs_ref[dst, 0:HALO, :] = jnp.where(at_start, jnp.zeros_like(prev_ref), prev_ref[...])
        lhs_ref[dst, HALO:HALO + tm, :] = main_ref[...]
        lhs_ref[dst, HALO + tm:, :] = jnp.where(at_end, jnp.zeros_like(next_ref), next_ref[...])

    def up(src, w_ref, u_ref):
        u = jnp.dot(lhs_ref[src], w_ref[...], preferred_element_type=F32)
        for lb in range(u_ref.shape[0]):
            u_ref[lb] = u[:, lb * LANES:(lb + 1) * LANES]

    def conv(u_ref, dw_ref, b_ref, lb):
        cols = slice(lb * LANES, (lb + 1) * LANES)
        rows = [u_ref[lb, pl.ds(HALO - 1 + k, half_rows, stride=2), :] for k in range(4)]
        w = [dw_ref[k:k + 1, cols] for k in range(3)]
        even = w[0] * rows[0] + w[1] * rows[1] + w[2] * rows[2] + b_ref[:, cols]
        odd = w[0] * rows[1] + w[1] * rows[2] + w[2] * rows[3] + b_ref[:, cols]
        return even, odd

    @pl.when(g == 0)
    def _():
        stage(0, 0)
        up(0, wg0_ref, ug_ref)

    @pl.when(j == nf - 2)
    def _():
        stage(jnp.minimum(i + 1, n_tiles - 1), 1 - slot)

    up(slot, wv_ref, uv_ref)
    for lb in range(ug_ref.shape[0]):
        gates = conv(ug_ref, dwg_ref, bg_ref, lb)
        vals = conv(uv_ref, dwv_ref, bv_ref, lb)
        for parity, (gate, val) in enumerate(zip(gates, vals)):
            y_ref[lb, pl.ds(parity, half_rows, stride=2), :] = gate * jax.nn.sigmoid(gate) * val
        o_ref[:, lb * LANES:(lb + 1) * LANES] = y_ref[lb].astype(o_ref.dtype)
    up(jnp.where(j == nf - 1, 1 - slot, slot), wg_ref, ug_ref)


def _ffn_up(xn, w_up, dw_w, dw_b, seq):
    t, d = xn.shape
    d_ff = w_up.shape[1] // 2
    tm = _pick(seq, (1024, 512, 256, 128))
    tf = _pick(d_ff, (512, 256, 128))
    nf = d_ff // tf
    n_tiles = t // tm
    assert dw_w.shape[0] == 3 and nf >= 3
    w_up = w_up.astype(BF16)
    per = tm // HALO
    last_halo = t // HALO - 1
    staged = lambda g: jnp.minimum((g + 2) // nf, n_tiles - 1)
    kern = functools.partial(_ffn_up_kernel, tm=tm, nf=nf, n_tiles=n_tiles,
                             tiles_per_seq=seq // tm)
    return pl.pallas_call(
        kern,
        grid=(n_tiles * nf,),
        in_specs=[pl.BlockSpec((HALO, d), lambda g: (jnp.maximum(staged(g) * per - 1, 0), 0)),
                  pl.BlockSpec((tm, d), lambda g: (staged(g), 0)),
                  pl.BlockSpec((HALO, d), lambda g: (jnp.minimum((staged(g) + 1) * per, last_halo), 0)),
                  pl.BlockSpec((d, tf), lambda g: (0, 0)),
                  pl.BlockSpec((d, tf), lambda g: (0, (g + 1) % nf)),
                  pl.BlockSpec((d, tf), lambda g: (0, g % nf + nf)),
                  pl.BlockSpec((3, tf), lambda g: (0, g % nf)),
                  pl.BlockSpec((3, tf), lambda g: (0, g % nf + nf)),
                  pl.BlockSpec((1, tf), lambda g: (0, g % nf)),
                  pl.BlockSpec((1, tf), lambda g: (0, g % nf + nf))],
        out_specs=pl.BlockSpec((tm, tf), lambda g: (g // nf, g % nf)),
        out_shape=jax.ShapeDtypeStruct((t, d_ff), BF16),
        scratch_shapes=[pltpu.VMEM((2, tm + 2 * HALO, d), BF16),
                        pltpu.VMEM((tf // LANES, tm + 2 * HALO, LANES), F32),
                        pltpu.VMEM((tf // LANES, tm + 2 * HALO, LANES), F32),
                        pltpu.VMEM((tf // LANES, tm, LANES), F32)],
        compiler_params=_params("arbitrary"),
        name="ffn_up",
    )(xn, xn, xn, w_up, w_up, w_up, dw_w, dw_w, dw_b, dw_b)


def _ffn_down_kernel(h_ref, wd_ref, x1_ref, fng_ref, o_ref, *, final_norm):
    y = x1_ref[...] + jnp.dot(h_ref[...], wd_ref[...], preferred_element_type=F32)
    o_ref[...] = _rms(y, fng_ref[...]) if final_norm else y


def _ffn_down(h, w_down, x1, fn_g, final_norm):
    t, d_ff = h.shape
    d = w_down.shape[1]
    tm = _pick(t, (512, 256, 128))
    kern = functools.partial(_ffn_down_kernel, final_norm=final_norm)
    return pl.pallas_call(
        kern,
        grid=(t // tm,),
        in_specs=[pl.BlockSpec((tm, d_ff), lambda i: (i, 0)),
                  pl.BlockSpec((d_ff, d), lambda i: (0, 0), pipeline_mode=pl.Buffered(1)),
                  pl.BlockSpec((tm, d), lambda i: (i, 0)),
                  pl.BlockSpec((1, d), lambda i: (0, 0))],
        out_specs=pl.BlockSpec((tm, d), lambda i: (i, 0)),
        out_shape=jax.ShapeDtypeStruct((t, d), F32),
        compiler_params=_params("parallel"),
        name="ffn_down",
    )(h, w_down, x1, fn_g)


def kernel(x, attn_norm_g, w_in, conv_dw_w, conv_dw_b, conv_ln_g, conv_ln_b, rpb, conv_out_g,
           na_out_g, w_out, ffn_norm_g, w_up, ffn_dw_w, ffn_dw_b, w_down, final_norm_g):
    batch, seq, d = x.shape
    depth = w_in.shape[0]
    c_dim = conv_dw_w.shape[-1]
    na_dim = na_out_g.shape[-1]
    assert seq % GRID_W == 0 and w_in.shape[-1] == 2 * c_dim + 3 * na_dim
    q_col = 2 * c_dim // LANES
    k_col = q_col + na_dim // LANES
    v_col = k_col + na_dim // LANES
    row = lambda a: a.reshape(1, -1).astype(F32)

    x2 = x.reshape(batch * seq, d)
    for l in range(depth):
        q_scale = jnp.ones((w_in.shape[-1],), F32).at[2 * c_dim:2 * c_dim + na_dim].set(
            HEAD_DIM ** -0.5 * LOG2E)
        proj = _in_proj(x2, row(attn_norm_g[l]), w_in[l] * q_scale)
        y_conv = _conv_branch(proj, conv_dw_w[l], row(conv_dw_b[l]), row(conv_ln_g[l]),
                              row(conv_ln_b[l]), row(conv_out_g[l]), seq)
        y_na = _natten(proj, _bias_table(rpb[l]), batch, seq, q_col, k_col, v_col, na_dim)
        x1, xn = _out_proj(y_conv, y_na, x2, row(na_out_g[l]), w_out[l].astype(BF16),
                           row(ffn_norm_g[l]))
        last = l == depth - 1
        h = _ffn_up(xn, w_up[l], ffn_dw_w[l], row(ffn_dw_b[l]), seq)
        x2 = _ffn_down(h, w_down[l].astype(BF16), x1, row(final_norm_g), final_norm=last)
    return x2.reshape(batch, seq, d)
```

```python
import functools

import numpy as np
import jax
import jax.numpy as jnp
from jax import lax
from jax.experimental import pallas as pl
from jax.experimental.pallas import tpu as pltpu

GRID_W = 64
WIN_H = 8
WIN_W = 16
HEAD_DIM = 64
EPS = 1e-6
MASKED = -1e30
LOG2E = 1.4426950408889634

LANES = 128
SUBLANES = 8
HALO = 16
VMEM_LIMIT = 56 * 1024 * 1024

F32 = jnp.float32
BF16 = jnp.bfloat16


def _pick(n, candidates):
    for c in candidates:
        if n % c == 0:
            return c
    raise ValueError(f"no tile in {candidates} divides {n}")


def _params(*semantics):
    return pltpu.CompilerParams(dimension_semantics=semantics, vmem_limit_bytes=VMEM_LIMIT)


def _rms(x, g):
    return x * lax.rsqrt(jnp.mean(x * x, axis=-1, keepdims=True) + EPS) * g


def _halo_specs(tm, width, n_rows):
    per = tm // HALO
    last = n_rows // HALO - 1
    prev = pl.BlockSpec((HALO, width), lambda i, *_: (jnp.maximum(i * per - 1, 0), 0))
    nxt = pl.BlockSpec((HALO, width), lambda i, *_: (jnp.minimum((i + 1) * per, last), 0))
    return prev, nxt


def _in_proj_kernel(x_ref, g_ref, w_ref, o_ref):
    xn = _rms(x_ref[...], g_ref[...]).astype(BF16)
    o_ref[...] = jnp.dot(xn, w_ref[...], preferred_element_type=F32).astype(o_ref.dtype)


def _in_proj(x2, g, w):
    t, d = x2.shape
    n = w.shape[1]
    tm = _pick(t, (512, 256, 128))
    return pl.pallas_call(
        _in_proj_kernel,
        grid=(t // tm,),
        in_specs=[pl.BlockSpec((tm, d), lambda i: (i, 0)),
                  pl.BlockSpec((1, d), lambda i: (0, 0)),
                  pl.BlockSpec((d, n), lambda i: (0, 0), pipeline_mode=pl.Buffered(1))],
        out_specs=pl.BlockSpec((tm, n), lambda i: (i, 0)),
        out_shape=jax.ShapeDtypeStruct((t, n), BF16),
        compiler_params=_params("parallel"),
        name="in_proj",
    )(x2, g, w.astype(BF16))


def _conv_kernel(prev_ref, main_ref, next_ref, dww_ref, dwb_ref,
                 o_ref, h_ref, c_ref, *, tm, tiles_per_seq, c_dim, taps, chunk):
    i = pl.program_id(0)
    at_start = (i % tiles_per_seq) == 0
    at_end = (i % tiles_per_seq) == tiles_per_seq - 1
    n_lb = c_dim // LANES
    base = HALO - taps // 2

    def glu(u_ref, lb):
        val = u_ref[:, lb * LANES:(lb + 1) * LANES].astype(F32)
        gate = u_ref[:, c_dim + lb * LANES:c_dim + (lb + 1) * LANES].astype(F32)
        return val * jax.nn.sigmoid(gate)

    for lb in range(n_lb):
        h_ref[lb, 0:HALO, :] = jnp.where(at_start, 0.0, glu(prev_ref, lb))
        h_ref[lb, HALO:HALO + tm, :] = glu(main_ref, lb)
        h_ref[lb, HALO + tm:, :] = jnp.where(at_end, 0.0, glu(next_ref, lb))

    def lane_block(lb, carry):
        w = dww_ref[lb]
        bias = jnp.broadcast_to(dwb_ref[lb], (chunk // 2, LANES))
        for r0 in range(0, tm, chunk):
            even, odd = bias, bias
            for m in range(taps + 1):
                rows = h_ref[lb, pl.ds(r0 + base + m, chunk // 2, stride=2), :]
                if m < taps:
                    even = even + w[m:m + 1] * rows
                if m >= 1:
                    odd = odd + w[m - 1:m] * rows
            c_ref[lb, pl.ds(r0, chunk // 2, stride=2), :] = even
            c_ref[lb, pl.ds(r0 + 1, chunk // 2, stride=2), :] = odd
        return carry

    lax.fori_loop(0, n_lb, lane_block, 0)
    for lb in range(n_lb):
        o_ref[:, lb * LANES:(lb + 1) * LANES] = c_ref[lb].astype(o_ref.dtype)


def _conv_branch(proj, dww, dwb, seq):
    t = proj.shape[0]
    taps, c_dim = dww.shape
    assert taps // 2 < HALO and c_dim % LANES == 0
    n_lb = c_dim // LANES
    tm = _pick(seq, (512, 256, 128))
    prev, nxt = _halo_specs(tm, 2 * c_dim, t)
    kern = functools.partial(_conv_kernel, tm=tm, tiles_per_seq=seq // tm, c_dim=c_dim,
                             taps=taps, chunk=128)
    dww_blocks = jnp.transpose(dww.reshape(taps, n_lb, LANES), (1, 0, 2))
    dwb_blocks = dwb.reshape(n_lb, 1, LANES)
    return pl.pallas_call(
        kern,
        grid=(t // tm,),
        in_specs=[prev, pl.BlockSpec((tm, 2 * c_dim), lambda i: (i, 0)), nxt,
                  pl.BlockSpec((n_lb, taps, LANES), lambda i: (0, 0, 0)),
                  pl.BlockSpec((n_lb, 1, LANES), lambda i: (0, 0, 0))],
        out_specs=pl.BlockSpec((tm, c_dim), lambda i: (i, 0)),
        out_shape=jax.ShapeDtypeStruct((t, c_dim), BF16),
        scratch_shapes=[pltpu.VMEM((n_lb, tm + 2 * HALO, LANES), F32),
                        pltpu.VMEM((n_lb, tm, LANES), F32)],
        compiler_params=_params("parallel"),
        name="conv_branch",
    )(proj, proj, proj, dww_blocks, dwb_blocks)


def _bias_table(rpb):
    n_heads = rpb.shape[0]
    cols = np.arange(GRID_W)
    col_start = np.clip(cols - WIN_W // 2, 0, GRID_W - WIN_W)
    kc = np.arange(GRID_W)
    rel = kc[None, :] - cols[:, None] + (WIN_W - 1)
    valid = (kc[None, :] >= col_start[:, None]) & (kc[None, :] < col_start[:, None] + WIN_W)
    lo = GRID_W - WIN_W
    a = jnp.pad(rpb.astype(F32) * LOG2E, ((0, 0), (0, 0), (lo, 2 * GRID_W - lo - (2 * WIN_W - 1))))
    t = jnp.tile(a, (1, 1, GRID_W))[..., :GRID_W * (2 * GRID_W - 1)]
    t = t.reshape(n_heads, -1, GRID_W, 2 * GRID_W - 1)[..., GRID_W - 1:]
    t = jnp.where(valid[None, None], t, MASKED)
    pairs = jnp.concatenate([t[:, :-1], t[:, 1:]], axis=-1)
    return jnp.transpose(pairs, (1, 0, 2, 3)).reshape(2 * WIN_H - 2, n_heads * GRID_W, 2 * GRID_W)


def _natten_kernel(q_ref, k_ref, v_ref, bias_ref, o_ref, *, rows, rows_per_step, rows_per_group,
                   lag):
    c = pl.program_id(2)
    lane = lax.broadcasted_iota(jnp.int32, (GRID_W, LANES), 1)
    first = lane < HEAD_DIM
    n_keys = WIN_H * GRID_W

    def scores(rr):
        r = c * rows_per_step + rr
        rs = jnp.clip(r - WIN_H // 2, 0, rows - WIN_H)
        d = rs - r + (WIN_H - 1)
        q = q_ref[pl.ds(pl.multiple_of(rr * GRID_W, GRID_W), GRID_W), :]
        zero = jnp.zeros_like(q)
        q2 = jnp.concatenate([jnp.where(first, q, zero), jnp.where(first, zero, q)], axis=0)
        kw = k_ref[pl.ds(pl.multiple_of(rs * GRID_W, GRID_W), n_keys), :]
        s = lax.dot_general(q2, kw, (((1,), (1,)), ((), ())), preferred_element_type=F32)
        bias = jnp.concatenate([bias_ref[d + 2 * m] for m in range(WIN_H // 2)], axis=1)
        return s + bias, rs

    def softmax(s):
        m = jnp.max(s, axis=-1, keepdims=True)
        p = jnp.exp2(s - m)
        return p.astype(BF16), jnp.sum(p, axis=-1, keepdims=True)

    def attend(rr, rs, p, l):
        vw = v_ref[pl.ds(pl.multiple_of(rs * GRID_W, GRID_W), n_keys), :]
        o = jnp.dot(p, vw, preferred_element_type=F32) / l
        out = jnp.where(first, o[:GRID_W], o[GRID_W:])
        o_ref[pl.ds(pl.multiple_of(rr * GRID_W, GRID_W), GRID_W), :] = out.astype(o_ref.dtype)

    def group(g, carry):
        pending = []
        for u in range(rows_per_group + lag):
            if u < rows_per_group:
                rr = g * rows_per_group + u
                s, rs = scores(rr)
                pending.append((rr, rs) + softmax(s))
            if u >= lag:
                attend(*pending[u - lag])
        return carry

    lax.fori_loop(0, rows_per_step // rows_per_group, group, 0)


def _natten(proj, bias, batch, seq, q_col, k_col, v_col, na_dim):
    t = proj.shape[0]
    rows = seq // GRID_W
    assert rows >= WIN_H and na_dim % LANES == 0 and LANES == 2 * HEAD_DIM == 2 * GRID_W
    pairs = na_dim // LANES
    rows_per_step = _pick(rows, (32, 16, 8))
    tq = rows_per_step * GRID_W
    chunks = seq // tq
    kern = functools.partial(_natten_kernel, rows=rows, rows_per_step=rows_per_step,
                             rows_per_group=32, lag=3)
    return pl.pallas_call(
        kern,
        grid=(batch, pairs, chunks),
        in_specs=[pl.BlockSpec((tq, LANES), lambda b, p, c: (b * chunks + c, q_col + p)),
                  pl.BlockSpec((seq, LANES), lambda b, p, c: (b, k_col + p)),
                  pl.BlockSpec((seq, LANES), lambda b, p, c: (b, v_col + p)),
                  pl.BlockSpec((2 * WIN_H - 2, LANES, LANES), lambda b, p, c: (0, p, 0))],
        out_specs=pl.BlockSpec((tq, LANES), lambda b, p, c: (b * chunks + c, p)),
        out_shape=jax.ShapeDtypeStruct((t, na_dim), BF16),
        compiler_params=_params("parallel", "parallel", "arbitrary"),
        name="natten",
    )(proj, proj, proj, bias)


def _out_proj_kernel(yc_ref, yn_ref, x_ref, lng_ref, lnb_ref, cg_ref, nag_ref, w_ref, fg_ref,
                     x1_ref, xn_ref, *, c_dim):
    c = yc_ref[...].astype(F32)
    dev = c - jnp.mean(c, axis=-1, keepdims=True)
    y = dev * lax.rsqrt(jnp.mean(dev * dev, axis=-1, keepdims=True) + EPS) * lng_ref[...] + lnb_ref[...]
    yc = _rms(y * jax.nn.sigmoid(y), cg_ref[...]).astype(BF16)
    na = _rms(yn_ref[...].astype(F32), nag_ref[...]).astype(BF16)
    acc = jnp.dot(yc, w_ref[0:c_dim, :], preferred_element_type=F32)
    acc = acc + jnp.dot(na, w_ref[c_dim:, :], preferred_element_type=F32)
    x1 = x_ref[...] + acc
    x1_ref[...] = x1
    xn_ref[...] = _rms(x1, fg_ref[...]).astype(xn_ref.dtype)


def _out_proj(y_conv, y_na, x2, ln_g, ln_b, conv_g, na_g, w_out, ffn_g):
    t, d = x2.shape
    c_dim, na_dim = y_conv.shape[1], y_na.shape[1]
    tm = _pick(t, (512, 256, 128))
    kern = functools.partial(_out_proj_kernel, c_dim=c_dim)
    cvec = pl.BlockSpec((1, c_dim), lambda i: (0, 0))
    return pl.pallas_call(
        kern,
        grid=(t // tm,),
        in_specs=[pl.BlockSpec((tm, c_dim), lambda i: (i, 0)),
                  pl.BlockSpec((tm, na_dim), lambda i: (i, 0)),
                  pl.BlockSpec((tm, d), lambda i: (i, 0)),
                  cvec, cvec, cvec,
                  pl.BlockSpec((1, na_dim), lambda i: (0, 0)),
                  pl.BlockSpec((c_dim + na_dim, d), lambda i: (0, 0)),
                  pl.BlockSpec((1, d), lambda i: (0, 0))],
        out_specs=[pl.BlockSpec((tm, d), lambda i: (i, 0)),
                   pl.BlockSpec((tm, d), lambda i: (i, 0))],
        out_shape=[jax.ShapeDtypeStruct((t, d), F32), jax.ShapeDtypeStruct((t, d), BF16)],
        compiler_params=_params("parallel"),
        name="out_proj",
    )(y_conv, y_na, x2, ln_g, ln_b, conv_g, na_g, w_out, ffn_g)


def _ffn_up_kernel(prev_ref, main_ref, next_ref, wg0_ref, wg_ref, wv_ref, dwg_ref, dwv_ref,
                   bg_ref, bv_ref, o_ref, lhs_ref, ug_ref, uv_ref, y_ref,
                   *, tm, nf, n_tiles, tiles_per_seq):
    g = pl.program_id(0)
    i = g // nf
    j = g % nf
    slot = i % 2
    half_rows = tm // 2

    def stage(tile, dst):
        at_start = (tile % tiles_per_seq) == 0
        at_end = (tile % tiles_per_seq) == tiles_per_seq - 1
        lhs_ref[dst, 0:HALO, :] = jnp.where(at_start, jnp.zeros_like(prev_ref), prev_ref[...])
        lhs_ref[dst, HALO:HALO + tm, :] = main_ref[...]
        lhs_ref[dst, HALO + tm:, :] = jnp.where(at_end, jnp.zeros_like(next_ref), next_ref[...])

    def up(src, w_ref, u_ref):
        u = jnp.dot(lhs_ref[src], w_ref[...], preferred_element_type=F32)
        for lb in range(u_ref.shape[0]):
            u_ref[lb] = u[:, lb * LANES:(lb + 1) * LANES]

    def conv(u_ref, dw_ref, b_ref, lb):
        cols = slice(lb * LANES, (lb + 1) * LANES)
        rows = [u_ref[lb, pl.ds(HALO - 1 + k, half_rows, stride=2), :] for k in range(4)]
        w = [dw_ref[k:k + 1, cols] for k in range(3)]
        even = w[0] * rows[0] + w[1] * rows[1] + w[2] * rows[2] + b_ref[:, cols]
        odd = w[0] * rows[1] + w[1] * rows[2] + w[2] * rows[3] + b_ref[:, cols]
        return even, odd

    @pl.when(g == 0)
    def _():
        stage(0, 0)
        up(0, wg0_ref, ug_ref)

    @pl.when(j == nf - 2)
    def _():
        stage(jnp.minimum(i + 1, n_tiles - 1), 1 - slot)

    up(slot, wv_ref, uv_ref)
    for lb in range(ug_ref.shape[0]):
        gates = conv(ug_ref, dwg_ref, bg_ref, lb)
        vals = conv(uv_ref, dwv_ref, bv_ref, lb)
        for parity, (gate, val) in enumerate(zip(gates, vals)):
            y_ref[lb, pl.ds(parity, half_rows, stride=2), :] = gate * jax.nn.sigmoid(gate) * val
        o_ref[:, lb * LANES:(lb + 1) * LANES] = y_ref[lb].astype(o_ref.dtype)
    up(jnp.where(j == nf - 1, 1 - slot, slot), wg_ref, ug_ref)


def _ffn_up(xn, w_up, dw_w, dw_b, seq):
    t, d = xn.shape
    d_ff = w_up.shape[1] // 2
    tm = _pick(seq, (1024, 512, 256, 128))
    tf = _pick(d_ff, (512, 256, 128))
    nf = d_ff // tf
    n_tiles = t // tm
    assert dw_w.shape[0] == 3 and nf >= 3
    w_up = w_up.astype(BF16)
    per = tm // HALO
    last_halo = t // HALO - 1
    staged = lambda g: jnp.minimum((g + 2) // nf, n_tiles - 1)
    kern = functools.partial(_ffn_up_kernel, tm=tm, nf=nf, n_tiles=n_tiles,
                             tiles_per_seq=seq // tm)
    return pl.pallas_call(
        kern,
        grid=(n_tiles * nf,),
        in_specs=[pl.BlockSpec((HALO, d), lambda g: (jnp.maximum(staged(g) * per - 1, 0), 0)),
                  pl.BlockSpec((tm, d), lambda g: (staged(g), 0)),
                  pl.BlockSpec((HALO, d), lambda g: (jnp.minimum((staged(g) + 1) * per, last_halo), 0)),
                  pl.BlockSpec((d, tf), lambda g: (0, 0)),
                  pl.BlockSpec((d, tf), lambda g: (0, (g + 1) % nf)),
                  pl.BlockSpec((d, tf), lambda g: (0, g % nf + nf)),
                  pl.BlockSpec((3, tf), lambda g: (0, g % nf)),
                  pl.BlockSpec((3, tf), lambda g: (0, g % nf + nf)),
                  pl.BlockSpec((1, tf), lambda g: (0, g % nf)),
                  pl.BlockSpec((1, tf), lambda g: (0, g % nf + nf))],
        out_specs=pl.BlockSpec((tm, tf), lambda g: (g // nf, g % nf)),
        out_shape=jax.ShapeDtypeStruct((t, d_ff), BF16),
        scratch_shapes=[pltpu.VMEM((2, tm + 2 * HALO, d), BF16),
                        pltpu.VMEM((tf // LANES, tm + 2 * HALO, LANES), F32),
                        pltpu.VMEM((tf // LANES, tm + 2 * HALO, LANES), F32),
                        pltpu.VMEM((tf // LANES, tm, LANES), F32)],
        compiler_params=_params("arbitrary"),
        name="ffn_up",
    )(xn, xn, xn, w_up, w_up, w_up, dw_w, dw_w, dw_b, dw_b)


def _ffn_down_kernel(h_ref, wd_ref, x1_ref, fng_ref, o_ref, *, final_norm):
    y = x1_ref[...] + jnp.dot(h_ref[...], wd_ref[...], preferred_element_type=F32)
    o_ref[...] = _rms(y, fng_ref[...]) if final_norm else y


def _ffn_down(h, w_down, x1, fn_g, final_norm):
    t, d_ff = h.shape
    d = w_down.shape[1]
    tm = _pick(t, (512, 256, 128))
    kern = functools.partial(_ffn_down_kernel, final_norm=final_norm)
    return pl.pallas_call(
        kern,
        grid=(t // tm,),
        in_specs=[pl.BlockSpec((tm, d_ff), lambda i: (i, 0)),
                  pl.BlockSpec((d_ff, d), lambda i: (0, 0), pipeline_mode=pl.Buffered(1)),
                  pl.BlockSpec((tm, d), lambda i: (i, 0)),
                  pl.BlockSpec((1, d), lambda i: (0, 0))],
        out_specs=pl.BlockSpec((tm, d), lambda i: (i, 0)),
        out_shape=jax.ShapeDtypeStruct((t, d), F32),
        compiler_params=_params("parallel"),
        name="ffn_down",
    )(h, w_down, x1, fn_g)


def kernel(x, attn_norm_g, w_in, conv_dw_w, conv_dw_b, conv_ln_g, conv_ln_b, rpb, conv_out_g,
           na_out_g, w_out, ffn_norm_g, w_up, ffn_dw_w, ffn_dw_b, w_down, final_norm_g):
    batch, seq, d = x.shape
    depth = w_in.shape[0]
    c_dim = conv_dw_w.shape[-1]
    na_dim = na_out_g.shape[-1]
    assert seq % GRID_W == 0 and w_in.shape[-1] == 2 * c_dim + 3 * na_dim
    q_col = 2 * c_dim // LANES
    k_col = q_col + na_dim // LANES
    v_col = k_col + na_dim // LANES
    row = lambda a: a.reshape(1, -1).astype(F32)

    x2 = x.reshape(batch * seq, d)
    for l in range(depth):
        q_scale = jnp.ones((w_in.shape[-1],), F32).at[2 * c_dim:2 * c_dim + na_dim].set(
            HEAD_DIM ** -0.5 * LOG2E)
        proj = _in_proj(x2, row(attn_norm_g[l]), w_in[l] * q_scale)
        y_conv = _conv_branch(proj, conv_dw_w[l], row(conv_dw_b[l]), seq)
        y_na = _natten(proj, _bias_table(rpb[l]), batch, seq, q_col, k_col, v_col, na_dim)
        x1, xn = _out_proj(y_conv, y_na, x2, row(conv_ln_g[l]), row(conv_ln_b[l]),
                           row(conv_out_g[l]), row(na_out_g[l]), w_out[l].astype(BF16),
                           row(ffn_norm_g[l]))
        last = l == depth - 1
        h = _ffn_up(xn, w_up[l], ffn_dw_w[l], row(ffn_dw_b[l]), seq)
        x2 = _ffn_down(h, w_down[l].astype(BF16), x1, row(final_norm_g), final_norm=last)
    return x2.reshape(batch, seq, d)
```
